```python
import functools
import jax, jax.numpy as jnp
from jax import lax
import numpy as np

D_MODEL = 2048
BATCH = 1
SEQ = 16384
DEPTH = 1
DEC_BATCH = 32
DEC_SEQ = 8
PAST_LEN = 16384
PAGE_SIZE = 128

A_WIDTH = D_MODEL // 2
A_HEAD_DIM = 64
A_HEADS = A_WIDTH // A_HEAD_DIM
DECAY_LORA = 64
AAA_LORA = 64
GATE_LORA = 128
A_COLS = 3 * A_WIDTH + DECAY_LORA + AAA_LORA + GATE_LORA
LNX_EPS = 64e-5
B_WIDTH = D_MODEL // 2
B_HEAD_DIM = 128
B_HEADS = B_WIDTH // B_HEAD_DIM
MOBA_BLOCK = 256
MOBA_TOPK = 3
QUERY_CHUNK = 32
B_COLS = 3 * B_WIDTH
MIX_WIDTH = A_WIDTH + B_WIDTH
N_IN = A_COLS + B_COLS + MIX_WIDTH
D_FF = ((8 * D_MODEL + 3 * 256 - 1) // (3 * 256)) * 256
RMS_EPS = 1e-6
NEG = -1e30

kernel_name = 'rwkv7_moba_hybrid_step'


def rms_norm(x, gain):
    xf = x.astype(jnp.float32)
    y = xf * lax.rsqrt(jnp.mean(xf * xf, axis=-1, keepdims=True) + RMS_EPS)
    return (y * gain.astype(jnp.float32)).astype(x.dtype)


def wkv7_scan(r, decay, k, v, a_vec, b_vec, s0):
    def step(s, inp):
        r_t, w_t, k_t, v_t, a_t, b_t = inp
        sa = jnp.einsum('bhvk,bhk->bhv', s, a_t)
        s = s * w_t[:, :, None, :] + sa[..., None] * b_t[:, :, None, :] + v_t[..., None] * k_t[:, :, None, :]
        return s, jnp.einsum('bhvk,bhk->bhv', s, r_t)
    xs = tuple(jnp.swapaxes(z, 0, 1) for z in (r, decay, k, v, a_vec, b_vec))
    s_final, ys = lax.scan(step, s0, xs)
    return jnp.swapaxes(ys, 0, 1), s_final


def rwkv7_branch(pa, shift0, wkv0, mu_shift, w0, w2, a0, a2, g2, k_k, k_a, r_k, lnx_w, lnx_b):
    bsz, t, _ = pa.shape
    f32 = jnp.float32
    prev = jnp.concatenate([shift0[:, None, :].astype(pa.dtype), pa[:, :-1]], axis=1)
    ps = (pa + (prev - pa) * mu_shift).astype(f32)
    o1, o2, o3 = A_WIDTH, 2 * A_WIDTH, 3 * A_WIDTH
    o4 = o3 + DECAY_LORA
    o5 = o4 + AAA_LORA
    r, k, v = ps[..., :o1], ps[..., o1:o2], ps[..., o2:o3]
    wl, al, gl = ps[..., o3:o4], ps[..., o4:o5], ps[..., o5:]
    w_log = -jax.nn.softplus(-(w0.astype(f32) + jnp.tanh(wl) @ w2.astype(f32))) - 0.5
    decay = jnp.exp(-jnp.exp(w_log))
    a = jax.nn.sigmoid(a0.astype(f32) + al @ a2.astype(f32))
    g = jax.nn.sigmoid(gl) @ g2.astype(f32)
    heads = lambda z: z.reshape(bsz, t, A_HEADS, A_HEAD_DIM)
    kk = heads(k * k_k.astype(f32))
    kk = kk * lax.rsqrt(jnp.maximum(jnp.sum(kk * kk, axis=-1, keepdims=True), 1e-24))
    k_mod = heads(k * (1.0 + (a - 1.0) * k_a.astype(f32)))
    a_h, r_h, v_h = heads(a), heads(r), heads(v)
    y, wkv_new = wkv7_scan(r_h, heads(decay), k_mod, v_h, -kk, kk * a_h, wkv0.astype(f32))
    mean = jnp.mean(y, axis=-1, keepdims=True)
    var = jnp.mean(jnp.square(y - mean), axis=-1, keepdims=True)
    y = ((y - mean) * lax.rsqrt(var + LNX_EPS)).reshape(bsz, t, A_WIDTH) * lnx_w.astype(f32) + lnx_b.astype(f32)
    bonus = jnp.sum(r_h * k_mod * r_k.astype(f32), axis=-1, keepdims=True) * v_h
    y = (y + bonus.reshape(bsz, t, A_WIDTH)) * g
    return y.astype(pa.dtype), pa[:, -1], wkv_new.astype(wkv0.dtype)


def moba_blocks(k, v):
    length = k.shape[0]
    nb = -(-length // MOBA_BLOCK)
    pad = nb * MOBA_BLOCK - length
    k = jnp.pad(k, ((0, pad), (0, 0), (0, 0)))
    v = jnp.pad(v, ((0, pad), (0, 0), (0, 0)))
    kb = k.reshape(nb, MOBA_BLOCK, B_HEADS, B_HEAD_DIM)
    vb = v.reshape(nb, MOBA_BLOCK, B_HEADS, B_HEAD_DIM)
    kmean = jnp.mean(kb.astype(jnp.float32), axis=1)
    return kb.transpose(2, 0, 1, 3), vb.transpose(2, 0, 1, 3), kmean


def moba_attend_chunk(q, q_pos, kh, vh, kmean):
    nq = q.shape[0]
    nb = kmean.shape[0]
    n_sel = max(1, min(MOBA_TOPK, nb - 1))
    own = q_pos // MOBA_BLOCK
    gate = jnp.einsum('qhd,nhd->qhn', q.astype(jnp.float32), kmean)
    fully_past = jnp.arange(nb)[None, None, :] < own[:, None, None]
    gate = jnp.where(fully_past, gate, NEG)
    _, sel = lax.top_k(gate, n_sel)
    sel_ok = sel < own[:, None, None]
    own_b = jnp.broadcast_to(own[:, None, None], (nq, B_HEADS, 1)).astype(sel.dtype)
    blocks = jnp.concatenate([sel, own_b], axis=-1)
    blk_ok = jnp.concatenate([sel_ok, jnp.ones((nq, B_HEADS, 1), bool)], axis=-1)
    h_idx = jnp.arange(B_HEADS)[None, :, None]
    kg = kh[h_idx, blocks]
    vg = vh[h_idx, blocks]
    key_pos = blocks[..., None] * MOBA_BLOCK + jnp.arange(MOBA_BLOCK)
    ok = blk_ok[..., None] & (key_pos <= q_pos[:, None, None, None])
    s = jnp.einsum('qhd,qhjsd->qhjs', q, kg, preferred_element_type=jnp.float32) * (B_HEAD_DIM ** -0.5)
    s = jnp.where(ok, s, NEG)
    p = jax.nn.softmax(s.reshape(nq, B_HEADS, -1), axis=-1).reshape(s.shape)
    return jnp.einsum('qhjs,qhjsd->qhd', p.astype(vg.dtype), vg)


def moba_prompt(q, k, v):
    t = q.shape[1]
    n_chunk = t // QUERY_CHUNK
    pos = jnp.arange(t).reshape(n_chunk, QUERY_CHUNK)

    def one_seq(qs, ks, vs):
        kh, vh, kmean = moba_blocks(ks, vs)
        qc = qs.reshape(n_chunk, QUERY_CHUNK, B_HEADS, B_HEAD_DIM)
        oc = lax.map(lambda a: moba_attend_chunk(a[0], a[1], kh, vh, kmean), (qc, pos))
        return oc.reshape(t, B_HEADS, B_HEAD_DIM)
    return jax.vmap(one_seq)(q, k, v)


def moba_sample(q, k, v, cache_k, cache_v, page_table, layer):
    t = q.shape[1]
    past = page_table.shape[1] * PAGE_SIZE
    pos = past + jnp.arange(t)

    def one_seq(args):
        qs, ks, vs, row = args
        pk = cache_k[layer, row].reshape(past, B_HEADS, B_HEAD_DIM)
        pv = cache_v[layer, row].reshape(past, B_HEADS, B_HEAD_DIM)
        kh, vh, kmean = moba_blocks(jnp.concatenate([pk, ks.astype(pk.dtype)], axis=0),
                                    jnp.concatenate([pv, vs.astype(pv.dtype)], axis=0))
        return moba_attend_chunk(qs, pos, kh, vh, kmean)
    return lax.map(one_seq, (q, k, v, page_table))


def trunk_layer(x, c, shift0, wkv0, moba_fn, w_ada, b_ada, g_pre_mix, g_post_mix, g_pre_ffn, g_post_ffn,
                w_in, mu_shift, w0, w2, a0, a2, g2, k_k, k_a, r_k, lnx_w, lnx_b, w_out, w_ffn_in, w_ffn_out):
    bsz, t, _ = x.shape
    mod = (jnp.einsum('bd,de->be', jax.nn.silu(c), w_ada) + b_ada).reshape(bsz, 6, 1, D_MODEL).astype(x.dtype)
    sh_m, sc_m, gt_m, sh_f, sc_f, gt_f = mod[:, 0], mod[:, 1], mod[:, 2], mod[:, 3], mod[:, 4], mod[:, 5]
    h = rms_norm(x, g_pre_mix) * (1.0 + sc_m) + sh_m
    proj = h @ w_in
    pa = proj[..., :A_COLS]
    pb = proj[..., A_COLS:A_COLS + B_COLS]
    pg = proj[..., A_COLS + B_COLS:]
    o_a, shift_new, wkv_new = rwkv7_branch(pa, shift0, wkv0, mu_shift, w0, w2, a0, a2, g2, k_k, k_a, r_k, lnx_w, lnx_b)
    qb = pb[..., :B_WIDTH].reshape(bsz, t, B_HEADS, B_HEAD_DIM)
    kb = pb[..., B_WIDTH:2 * B_WIDTH].reshape(bsz, t, B_HEADS, B_HEAD_DIM)
    vb = pb[..., 2 * B_WIDTH:].reshape(bsz, t, B_HEADS, B_HEAD_DIM)
    o_b = moba_fn(qb, kb, vb).reshape(bsz, t, B_WIDTH).astype(x.dtype)
    merged = jnp.concatenate([o_a, o_b], axis=-1) * jax.nn.sigmoid(pg)
    x = x + gt_m * rms_norm(merged @ w_out, g_post_mix)
    h = rms_norm(x, g_pre_ffn) * (1.0 + sc_f) + sh_f
    gu = h @ w_ffn_in
    f = (jax.nn.silu(gu[..., :D_FF]) * gu[..., D_FF:]) @ w_ffn_out
    x = x + gt_f * rms_norm(f, g_post_ffn)
    return x, shift_new, wkv_new, kb, vb


def setup_inputs(seed: int = 0) -> dict:
    key = jax.random.key(seed)
    ks = iter(jax.random.split(key, 40))
    f32 = jnp.float32

    def nrm(shape, scale):
        return scale * jax.random.normal(next(ks), shape, f32)
    n_pages = PAST_LEN // PAGE_SIZE
    n_used = DEC_BATCH * n_pages
    n_pool = n_used + n_used // 4
    perm = jax.random.permutation(next(ks), n_pool).astype(jnp.int32)
    page_table = perm[:n_used].reshape(DEC_BATCH, n_pages)
    L = DEPTH
    return {
        'x_prompt': nrm((BATCH, SEQ, D_MODEL), 1.0),
        'x_sample': nrm((DEC_BATCH, DEC_SEQ, D_MODEL), 1.0),
        'cache_k': nrm((L, n_pool, PAGE_SIZE, B_HEADS, B_HEAD_DIM), 1.0),
        'cache_v': nrm((L, n_pool, PAGE_SIZE, B_HEADS, B_HEAD_DIM), 1.0),
        'page_table': page_table,
        'state_wkv': nrm((L, DEC_BATCH, A_HEADS, A_HEAD_DIM, A_HEAD_DIM), 0.5),
        'state_shift': nrm((L, DEC_BATCH, A_COLS), 1.0),
        'c_prompt': nrm((BATCH, D_MODEL), 1.0),
        'c_sample': nrm((DEC_BATCH, D_MODEL), 1.0),
        'w_ada': nrm((L, D_MODEL, 6 * D_MODEL), D_MODEL ** -0.5),
        'b_ada': nrm((L, 6 * D_MODEL), 0.02),
        'g_pre_mix': 1.0 + nrm((L, D_MODEL), 0.05),
        'g_post_mix': 1.0 + nrm((L, D_MODEL), 0.05),
        'g_pre_ffn': 1.0 + nrm((L, D_MODEL), 0.05),
        'g_post_ffn': 1.0 + nrm((L, D_MODEL), 0.05),
        'w_in': nrm((L, D_MODEL, N_IN), D_MODEL ** -0.5),
        'mu_shift': jax.random.uniform(next(ks), (L, A_COLS), f32),
        'w0': nrm((L, A_WIDTH), 0.5),
        'w2': nrm((L, DECAY_LORA, A_WIDTH), 0.5 * DECAY_LORA ** -0.5),
        'a0': nrm((L, A_WIDTH), 0.5),
        'a2': nrm((L, AAA_LORA, A_WIDTH), 0.5 * AAA_LORA ** -0.5),
        'g2': nrm((L, GATE_LORA, A_WIDTH), GATE_LORA ** -0.5),
        'k_k': 0.85 + nrm((L, A_WIDTH), 0.05),
        'k_a': 1.0 + nrm((L, A_WIDTH), 0.05),
        'r_k': nrm((L, A_HEADS, A_HEAD_DIM), 0.1),
        'lnx_w': 1.0 + nrm((L, A_WIDTH), 0.05),
        'lnx_b': nrm((L, A_WIDTH), 0.01),
        'w_out': nrm((L, MIX_WIDTH, D_MODEL), MIX_WIDTH ** -0.5),
        'w_ffn_in': nrm((L, D_MODEL, 2 * D_FF), D_MODEL ** -0.5),
        'w_ffn_out': nrm((L, D_FF, D_MODEL), D_FF ** -0.5),
    }


def reference(x_prompt, x_sample, cache_k, cache_v, page_table, state_wkv, state_shift, c_prompt, c_sample,
              w_ada, b_ada, g_pre_mix, g_post_mix, g_pre_ffn, g_post_ffn, w_in, mu_shift, w0, w2, a0, a2, g2,
              k_k, k_a, r_k, lnx_w, lnx_b, w_out, w_ffn_in, w_ffn_out):
    yp, ys = x_prompt, x_sample
    kp_l, vp_l, wkvp_l, shp_l = [], [], [], []
    ks_l, vs_l, wkvs_l, shs_l = [], [], [], []
    zero_shift = jnp.zeros((x_prompt.shape[0], A_COLS), x_prompt.dtype)
    zero_wkv = jnp.zeros((x_prompt.shape[0], A_HEADS, A_HEAD_DIM, A_HEAD_DIM), x_prompt.dtype)
    for l in range(DEPTH):
        lw = (w_ada[l], b_ada[l], g_pre_mix[l], g_post_mix[l], g_pre_ffn[l], g_post_ffn[l], w_in[l], mu_shift[l],
              w0[l], w2[l], a0[l], a2[l], g2[l], k_k[l], k_a[l], r_k[l], lnx_w[l], lnx_b[l], w_out[l],
              w_ffn_in[l], w_ffn_out[l])
        yp, shp, wkvp, kp, vp = trunk_layer(yp, c_prompt, zero_shift, zero_wkv, moba_prompt, *lw)
        sample_fn = functools.partial(moba_sample, cache_k=cache_k, cache_v=cache_v, page_table=page_table, layer=l)
        ys, shs, wkvs, ksm, vsm = trunk_layer(ys, c_sample, state_shift[l], state_wkv[l], sample_fn, *lw)
        kp_l.append(kp); vp_l.append(vp); wkvp_l.append(wkvp); shp_l.append(shp)
        ks_l.append(ksm); vs_l.append(vsm); wkvs_l.append(wkvs); shs_l.append(shs)
    k_prompt = jnp.stack(kp_l)
    v_prompt = jnp.stack(vp_l)
    wkv_prompt = jnp.stack(wkvp_l)
    shift_prompt = jnp.stack(shp_l)
    k_sample = jnp.stack(ks_l)
    v_sample = jnp.stack(vs_l)
    wkv_sample = jnp.stack(wkvs_l)
    shift_sample = jnp.stack(shs_l)
    return (yp, ys, k_prompt, v_prompt, wkv_prompt, shift_prompt, k_sample, v_sample, wkv_sample, shift_sample)
```

```python
import functools

import numpy as np
import jax
import jax.numpy as jnp
from jax import lax
from jax.experimental import pallas as pl
from jax.experimental.pallas import tpu as pltpu

F32 = jnp.float32
BF16 = jnp.bfloat16

A_HEAD_DIM = 64
DECAY_LORA = 64
AAA_LORA = 64
GATE_LORA = 128
B_HEAD_DIM = 128
MOBA_BLOCK = 256
MOBA_TOPK = 3
PAGE_SIZE = 128
LNX_EPS = 64e-5
RMS_EPS = 1e-6
NEG = -1e30

LANES = 128
SUBLANES = 8
VMEM_LIMIT = 56 * 1024 * 1024


def _cparams(sem):
    return pltpu.CompilerParams(dimension_semantics=sem, vmem_limit_bytes=VMEM_LIMIT)


def _split_bf16(x):
    hi = x.astype(BF16)
    lo = (x - hi.astype(F32)).astype(BF16)
    return hi, lo


def _seg_sum128(x, m2):
    hi, lo = _split_bf16(x)
    return jnp.dot(jnp.concatenate([hi, lo], axis=1), m2, preferred_element_type=F32)


def _seg_sum(x, m2):
    n = x.shape[1] // LANES
    return jnp.concatenate([_seg_sum128(x[:, c * LANES:(c + 1) * LANES], m2) for c in range(n)], axis=1)


def _block_ones():
    i = np.arange(LANES)
    m = (i[:, None] // A_HEAD_DIM == i[None, :] // A_HEAD_DIM).astype(np.float32)
    return jnp.asarray(np.concatenate([m, m], axis=0), dtype=BF16)


def _ada_kernel(c_ref, w_ref, b_ref, o_ref):
    c = c_ref[...]
    s = (c * jax.nn.sigmoid(c)).astype(BF16)
    o_ref[...] = jnp.dot(s, w_ref[...].astype(BF16), preferred_element_type=F32) + b_ref[...]


def _ada(c, w_ada, b_ada):
    m, d = c.shape
    n = w_ada.shape[1]
    tn = 1024
    return pl.pallas_call(
        _ada_kernel,
        grid=(n // tn,),
        in_specs=[pl.BlockSpec((m, d), lambda j: (0, 0)),
                  pl.BlockSpec((d, tn), lambda j: (0, j)),
                  pl.BlockSpec((1, tn), lambda j: (0, j))],
        out_specs=pl.BlockSpec((m, tn), lambda j: (0, j)),
        out_shape=jax.ShapeDtypeStruct((m, n), F32),
        compiler_params=_cparams(("arbitrary",)),
        name="ada_mod",
    )(c, w_ada, b_ada.reshape(1, n))


def _norm_mod(x, g, sc, sh):
    ms = jnp.mean(x * x, axis=-1, keepdims=True)
    y = x * lax.rsqrt(ms + RMS_EPS) * g
    return y * (1.0 + sc) + sh


def _nmm_kernel(x_ref, g_ref, sc_ref, sh_ref, w_ref, o_ref, h_ref):
    gb, tm, d = x_ref.shape

    @pl.when(pl.program_id(2) == 0)
    def _():
        h = _norm_mod(x_ref[...], g_ref[...], sc_ref[...], sh_ref[...])
        h_ref[...] = h.reshape(gb * tm, d).astype(BF16)

    o = jnp.dot(h_ref[...], w_ref[...], preferred_element_type=F32)
    o_ref[...] = o.reshape(o_ref.shape).astype(o_ref.dtype)


def _norm_mod_matmul(x, g, sc, sh, w, gb, tm, tn, out_dtype=F32):
    G, T, D = x.shape
    N = w.shape[1]
    return pl.pallas_call(
        _nmm_kernel,
        grid=(G // gb, T // tm, N // tn),
        in_specs=[pl.BlockSpec((gb, tm, D), lambda a, i, j: (a, i, 0)),
                  pl.BlockSpec((1, 1, D), lambda a, i, j: (0, 0, 0)),
                  pl.BlockSpec((gb, 1, D), lambda a, i, j: (a, 0, 0)),
                  pl.BlockSpec((gb, 1, D), lambda a, i, j: (a, 0, 0)),
                  pl.BlockSpec((D, tn), lambda a, i, j: (0, j))],
        out_specs=pl.BlockSpec((gb, tm, tn), lambda a, i, j: (a, i, j)),
        out_shape=jax.ShapeDtypeStruct((G, T, N), out_dtype),
        scratch_shapes=[pltpu.VMEM((gb * tm, D), BF16)],
        compiler_params=_cparams(("arbitrary", "arbitrary", "arbitrary")),
        name="norm_mod_matmul",
    )(x, g, sc, sh, w)


def _softplus(z):
    return jnp.maximum(z, 0.0) + jnp.log(1.0 + jnp.exp(-jnp.abs(z)))


def _prep_kernel(pa_ref, sh0_ref, mu_ref, w0_ref, a0_ref, kk_ref, ka_ref, rk_ref, wwa_ref, g2_ref, m2_ref,
                 r_o, w_o, k_o, v_o, a_o, b_o, g_o, bon_o, last_o, carry_ref):
    tm = pa_ref.shape[1]
    aw = r_o.shape[2]

    @pl.when(pl.program_id(1) == 0)
    def _():
        carry_ref[...] = sh0_ref[0]

    pa = pa_ref[0]
    rolled = pltpu.roll(pa, 1, axis=0)
    row = lax.broadcasted_iota(jnp.int32, (tm, 1), 0)
    prev = jnp.where(row == 0, carry_ref[...], rolled)
    last = pa[tm - 1:tm, :]
    carry_ref[...] = last
    last_o[0] = last

    ps = pa + (prev - pa) * mu_ref[...]
    r = ps[:, 0:aw]
    k = ps[:, aw:2 * aw]
    v = ps[:, 2 * aw:3 * aw]
    o3 = 3 * aw
    x_wa = ps[:, o3:o3 + DECAY_LORA + AAA_LORA]
    gl = ps[:, o3 + DECAY_LORA + AAA_LORA:]
    lane = lax.broadcasted_iota(jnp.int32, x_wa.shape, 1)
    z = jnp.where(lane < DECAY_LORA, jnp.tanh(x_wa), x_wa).astype(BF16)
    wa = jnp.dot(z, wwa_ref[...], preferred_element_type=F32)
    lw = wa[:, :aw]
    la = wa[:, aw:]
    g = jnp.dot(jax.nn.sigmoid(gl).astype(BF16), g2_ref[...], preferred_element_type=F32)

    w_log = -_softplus(-(w0_ref[...] + lw)) - 0.5
    decay = jnp.exp(-jnp.exp(w_log))
    a = jax.nn.sigmoid(a0_ref[...] + la)
    m2 = m2_ref[...]
    kk = k * kk_ref[...]
    kk = kk * lax.rsqrt(jnp.maximum(_seg_sum(kk * kk, m2), 1e-24))
    k_mod = k * (1.0 + (a - 1.0) * ka_ref[...])
    bonus = _seg_sum(r * k_mod * rk_ref[...], m2) * v

    r_o[0] = r
    w_o[0] = decay
    k_o[0] = k_mod
    v_o[0] = v
    a_o[0] = -kk
    b_o[0] = kk * a
    g_o[0] = g
    bon_o[0] = bonus


def _rwkv_prep(pa, shift0, mu, w0, a0, k_k, k_a, r_k, wwa, g2, m2, tm):
    G, T, AC = pa.shape
    aw = w0.shape[-1]
    vec = lambda n: pl.BlockSpec((1, n), lambda a, i: (0, 0))
    full = lambda s: pl.BlockSpec(s, lambda a, i: (0, 0))
    big = pl.BlockSpec((1, tm, aw), lambda a, i: (a, i, 0))
    outs = [jax.ShapeDtypeStruct((G, T, aw), F32)] * 8 + [jax.ShapeDtypeStruct((G, 1, AC), F32)]
    return pl.pallas_call(
        _prep_kernel,
        grid=(G, T // tm),
        in_specs=[pl.BlockSpec((1, tm, AC), lambda a, i: (a, i, 0)),
                  pl.BlockSpec((1, 1, AC), lambda a, i: (a, 0, 0)),
                  vec(AC), vec(aw), vec(aw), vec(aw), vec(aw), vec(aw),
                  full(wwa.shape), full(g2.shape), full(m2.shape)],
        out_specs=[big] * 8 + [pl.BlockSpec((1, 1, AC), lambda a, i: (a, 0, 0))],
        out_shape=outs,
        scratch_shapes=[pltpu.VMEM((1, AC), F32)],
        compiler_params=_cparams(("arbitrary", "arbitrary")),
        name="rwkv_prep",
    )(pa, shift0, mu, w0, a0, k_k, k_a, r_k, wwa, g2, m2)


def _scan_kernel(r_ref, w_ref, k_ref, v_ref, a_ref, b_ref, s0_ref, m2_ref, y_ref, sf_ref, s_ref):
    C = r_ref.shape[1]
    npair = r_ref.shape[2]
    rows = npair * A_HEAD_DIM

    @pl.when(pl.program_id(1) == 0)
    def _():
        s_ref[...] = s0_ref[0]

    y_ref[...] = jnp.zeros(y_ref.shape, F32)
    m2 = m2_ref[...]
    m1 = m2[:LANES]
    rowv = lax.broadcasted_iota(jnp.int32, (rows, LANES), 0) % A_HEAD_DIM
    lane = lax.broadcasted_iota(jnp.int32, (rows, LANES), 1)
    diag = (lane % A_HEAD_DIM) == rowv

    def bc(ref, t):
        return jnp.concatenate(
            [jnp.broadcast_to(ref[0, t, p:p + 1, :], (A_HEAD_DIM, LANES)) for p in range(npair)], axis=0)

    def step(t, carry):
        S = s_ref[...]
        sa = _seg_sum128(S * bc(a_ref, t), m2)
        vb = _seg_sum128(jnp.where(diag, bc(v_ref, t), 0.0), m2)
        S = S * bc(w_ref, t) + sa * bc(b_ref, t) + vb * bc(k_ref, t)
        s_ref[...] = S
        yb = jnp.dot((S * bc(r_ref, t)).astype(BF16), m1, preferred_element_type=F32)
        y_ref[0, 0] = jnp.where((lane % A_HEAD_DIM) == t, yb, y_ref[0, 0])
        return carry

    lax.fori_loop(0, C, step, 0)
    sf_ref[0] = s_ref[...]


def _wkv_scan(r, w, k, v, a, b, s0, m2, C):
    G, T, npair, _ = r.shape
    rows = npair * A_HEAD_DIM
    seq = pl.BlockSpec((1, C, npair, LANES), lambda g, c: (g, c, 0, 0))
    st = pl.BlockSpec((1, rows, LANES), lambda g, c: (g, 0, 0))
    return pl.pallas_call(
        _scan_kernel,
        grid=(G, T // C),
        in_specs=[seq] * 6 + [st, pl.BlockSpec(m2.shape, lambda g, c: (0, 0))],
        out_specs=[pl.BlockSpec((1, 1, rows, LANES), lambda g, c: (g, c, 0, 0)), st],
        out_shape=[jax.ShapeDtypeStruct((G, T // C, rows, LANES), F32),
                   jax.ShapeDtypeStruct((G, rows, LANES), F32)],
        scratch_shapes=[pltpu.VMEM((rows, LANES), F32)],
        compiler_params=_cparams(("arbitrary", "arbitrary")),
        name="wkv_scan",
    )(r, w, k, v, a, b, s0, m2)


def _post_kernel(yt_ref, g_ref, bon_ref, lw_ref, lb_ref, m2_ref, o_ref):
    C = o_ref.shape[1]
    rows = yt_ref.shape[2]
    lane = lax.broadcasted_iota(jnp.int32, (C, LANES), 1)
    lo_half = lane < A_HEAD_DIM
    chunks = []
    for pp in range(rows // LANES):
        rt = yt_ref[0, 0, pp * LANES:(pp + 1) * LANES, :].T
        ra = rt[0:C]
        rb = rt[A_HEAD_DIM:A_HEAD_DIM + C]
        chunks.append(jnp.where(lo_half, ra, pltpu.roll(rb, A_HEAD_DIM, axis=1)))
        chunks.append(jnp.where(lo_half, pltpu.roll(ra, A_HEAD_DIM, axis=1), rb))
    y = jnp.concatenate(chunks, axis=1)
    m2 = m2_ref[...]
    inv = 1.0 / A_HEAD_DIM
    mean = _seg_sum(y, m2) * inv
    yc = y - mean
    var = _seg_sum(yc * yc, m2) * inv
    yn = yc * lax.rsqrt(var + LNX_EPS) * lw_ref[...] + lb_ref[...]
    o_ref[0] = ((yn + bon_ref[0]) * g_ref[0]).astype(o_ref.dtype)


def _rwkv_post(yt, g, bon, lnx_w, lnx_b, m2):
    G, NC, rows, _ = yt.shape
    T, aw = g.shape[1], g.shape[2]
    C = T // NC
    big = pl.BlockSpec((1, C, aw), lambda a, c: (a, c, 0))
    vec = pl.BlockSpec((1, aw), lambda a, c: (0, 0))
    return pl.pallas_call(
        _post_kernel,
        grid=(G, NC),
        in_specs=[pl.BlockSpec((1, 1, rows, LANES), lambda a, c: (a, c, 0, 0)), big, big, vec, vec,
                  pl.BlockSpec(m2.shape, lambda a, c: (0, 0))],
        out_specs=big,
        out_shape=jax.ShapeDtypeStruct((G, T, aw), F32),
        compiler_params=_cparams(("arbitrary", "arbitrary")),
        name="rwkv_post",
    )(yt, g, bon, lnx_w, lnx_b, m2)


def _moba_prep_kernel(k_ref, v_ref, kb_ref, vt_ref, km_ref):
    k = k_ref[...]
    kb_ref[...] = k.astype(BF16)
    km_ref[0] = jnp.mean(k, axis=0, keepdims=True)
    vt_ref[...] = v_ref[...].T.astype(BF16)


def _moba_prep(k, v):
    T, W = k.shape
    nb = T // MOBA_BLOCK
    blk = pl.BlockSpec((MOBA_BLOCK, W), lambda n: (n, 0))
    return pl.pallas_call(
        _moba_prep_kernel,
        grid=(nb,),
        in_specs=[blk, blk],
        out_specs=[blk, pl.BlockSpec((W, MOBA_BLOCK), lambda n: (0, n)),
                   pl.BlockSpec((1, 1, W), lambda n: (n, 0, 0))],
        out_shape=[jax.ShapeDtypeStruct((T, W), BF16), jax.ShapeDtypeStruct((W, T), BF16),
                   jax.ShapeDtypeStruct((nb, 1, W), F32)],
        compiler_params=_cparams(("arbitrary",)),
        name="moba_prep",
    )(k, v)


def _select_topk(gate, valid, axis):
    idx = lax.broadcasted_iota(jnp.int32, gate.shape, axis)
    g = jnp.where(valid, gate, NEG)
    sel = jnp.zeros(gate.shape, jnp.bool_)
    for _ in range(MOBA_TOPK):
        m = jnp.max(g, axis=axis, keepdims=True)
        first = jnp.min(jnp.where(g == m, idx, jnp.int32(2 ** 30)), axis=axis, keepdims=True)
        pick = idx == first
        sel = jnp.logical_or(sel, pick)
        g = jnp.where(pick, -jnp.inf, g)
    return jnp.logical_and(sel, valid)


def _moba_prompt_kernel(q_ref, k_ref, vt_ref, km_ref, o_ref, sel_ref):
    i = pl.program_id(1)
    tq = q_ref.shape[0]
    nb = km_ref.shape[0]
    q = q_ref[...]
    qb = (q * (B_HEAD_DIM ** -0.5)).astype(BF16)
    nt = (((1,), (1,)), ((), ()))

    gate_t = lax.dot_general(km_ref[...], q, nt, precision=lax.Precision.HIGHEST,
                             preferred_element_type=F32)
    blk = lax.broadcasted_iota(jnp.int32, (nb, tq), 0)
    sel_ref[...] = _select_topk(gate_t, blk < i, 0).astype(F32)

    start = pl.multiple_of(i * MOBA_BLOCK, MOBA_BLOCK)
    s_t = lax.dot_general(k_ref[pl.ds(start, MOBA_BLOCK), :], qb, nt, preferred_element_type=F32)
    key = lax.broadcasted_iota(jnp.int32, (MOBA_BLOCK, tq), 0)
    qi = lax.broadcasted_iota(jnp.int32, (MOBA_BLOCK, tq), 1)
    s_t = jnp.where(key <= qi, s_t, NEG)
    m0 = jnp.max(s_t, axis=0, keepdims=True)
    p = jnp.exp(s_t - m0)
    l0 = jnp.sum(p, axis=0, keepdims=True)
    acc0 = jnp.dot(vt_ref[:, pl.ds(start, MOBA_BLOCK)], p.astype(BF16), preferred_element_type=F32)

    def body(n, carry):
        m, l, acc = carry
        st = pl.multiple_of(n * MOBA_BLOCK, MOBA_BLOCK)
        s_t = lax.dot_general(k_ref[pl.ds(st, MOBA_BLOCK), :], qb, nt, preferred_element_type=F32)
        ok = sel_ref[pl.ds(n, 1), :] > 0.5
        s_t = jnp.where(ok, s_t, NEG)
        m_new = jnp.maximum(m, jnp.max(s_t, axis=0, keepdims=True))
        alpha = jnp.exp(m - m_new)
        p = jnp.exp(s_t - m_new)
        l = alpha * l + jnp.sum(p, axis=0, keepdims=True)
        acc = alpha * acc + jnp.dot(vt_ref[:, pl.ds(st, MOBA_BLOCK)], p.astype(BF16),
                                    preferred_element_type=F32)
        return m_new, l, acc

    m, l, acc = lax.fori_loop(0, i, body, (m0, l0, acc0))
    o_ref[...] = (acc / l).T


def _moba_prompt(q, kb, vt, kmean):
    T, W = q.shape
    nh = W // B_HEAD_DIM
    nb = T // MOBA_BLOCK
    return pl.pallas_call(
        _moba_prompt_kernel,
        grid=(nh, nb),
        in_specs=[pl.BlockSpec((MOBA_BLOCK, B_HEAD_DIM), lambda h, i: (i, h)),
                  pl.BlockSpec((T, B_HEAD_DIM), lambda h, i: (0, h)),
                  pl.BlockSpec((B_HEAD_DIM, T), lambda h, i: (h, 0)),
                  pl.BlockSpec((nb, B_HEAD_DIM), lambda h, i: (0, h))],
        out_specs=pl.BlockSpec((MOBA_BLOCK, B_HEAD_DIM), lambda h, i: (i, h)),
        out_shape=jax.ShapeDtypeStruct((T, W), F32),
        scratch_shapes=[pltpu.VMEM((nb, MOBA_BLOCK), F32)],
        compiler_params=_cparams(("arbitrary", "arbitrary")),
        name="moba_prompt",
    )(q, kb, vt, kmean)


SAMPLE_PAGES_PER_STEP = 8


def _moba_sample_kernel(pt_ref, q_ref, kn_ref, vn_ref, *rest):
    P = SAMPLE_PAGES_PER_STEP
    kp = rest[:P]
    vp = rest[P:2 * P]
    o_ref = rest[2 * P]
    qbd_ref, km_ref, m_ref, l_ref, oacc_ref = rest[2 * P + 1:]
    s = pl.program_id(1)
    ns = pl.num_programs(1)
    tq, W = q_ref.shape[1], q_ref.shape[2]
    nh = W // B_HEAD_DIM
    hq = nh * tq
    nbp = km_ref.shape[0]
    bps = P * PAGE_SIZE // MOBA_BLOCK
    nt = (((1,), (1,)), ((), ()))
    rowh = lax.broadcasted_iota(jnp.int32, (hq, W), 0) // tq
    colh = lax.broadcasted_iota(jnp.int32, (hq, W), 1) // B_HEAD_DIM

    @pl.when(s == 0)
    def _():
        q = q_ref[0]
        qbd_ref[...] = jnp.where(rowh == colh, jnp.concatenate([q] * nh, axis=0), 0.0)

    qbd = qbd_ref[...]
    qbd_b = (qbd * (B_HEAD_DIM ** -0.5)).astype(BF16)

    def partial(kblk_b, vblk_b, mask):
        sc = lax.dot_general(qbd_b, kblk_b, nt, preferred_element_type=F32)
        if mask is not None:
            sc = jnp.where(mask, sc, NEG)
        m = jnp.max(sc, axis=1, keepdims=True)
        p = jnp.exp(sc - m)
        l = jnp.sum(p, axis=1, keepdims=True)
        of = jnp.dot(p.astype(BF16), vblk_b, preferred_element_type=F32)
        o = jnp.concatenate([of[h * tq:(h + 1) * tq, h * B_HEAD_DIM:(h + 1) * B_HEAD_DIM] for h in range(nh)],
                            axis=0)
        return jnp.broadcast_to(m, (hq, B_HEAD_DIM)), jnp.broadcast_to(l, (hq, B_HEAD_DIM)), o

    ppb = MOBA_BLOCK // PAGE_SIZE
    for j in range(bps):
        kblk = jnp.concatenate([kp[j * ppb + u][0] for u in range(ppb)], axis=0)
        vblk = jnp.concatenate([vp[j * ppb + u][0] for u in range(ppb)], axis=0)
        n = s * bps + j
        km_ref[pl.ds(n, 1), :] = jnp.mean(kblk, axis=0, keepdims=True)
        m, l, o = partial(kblk.astype(BF16), vblk.astype(BF16), None)
        m_ref[n] = m
        l_ref[n] = l
        oacc_ref[n] = o

    @pl.when(s == ns - 1)
    def _():
        pad = jnp.zeros((LANES - tq, W), F32)
        kn = jnp.concatenate([kn_ref[0], pad], axis=0).astype(BF16)
        vn = jnp.concatenate([vn_ref[0], pad], axis=0).astype(BF16)
        key = lax.broadcasted_iota(jnp.int32, (hq, LANES), 1)
        qrow = lax.broadcasted_iota(jnp.int32, (hq, LANES), 0) % tq
        m_own, l_own, o_own = partial(kn, vn, key <= qrow)

        gate = lax.dot_general(qbd, km_ref[...], nt, precision=lax.Precision.HIGHEST,
                               preferred_element_type=F32)
        sel = _select_topk(gate, jnp.ones(gate.shape, jnp.bool_), 1).astype(F32)
        mx = m_own
        for n in range(nbp):
            sn = sel[:, n:n + 1] > 0.5
            mx = jnp.where(sn, jnp.maximum(mx, m_ref[n]), mx)
        w_own = jnp.exp(m_own - mx)
        lsum = w_own * l_own
        osum = w_own * o_own
        for n in range(nbp):
            sn = sel[:, n:n + 1] > 0.5
            wn = jnp.where(sn, jnp.exp(m_ref[n] - mx), 0.0)
            lsum = lsum + wn * l_ref[n]
            osum = osum + wn * oacc_ref[n]
        o_ref[0] = osum / lsum


def _moba_sample(q, kn, vn, cache_k, cache_v, page_table):
    B, tq, W = q.shape
    nh = W // B_HEAD_DIM
    hq = nh * tq
    n_pages = page_table.shape[1]
    P = SAMPLE_PAGES_PER_STEP
    nbp = n_pages * PAGE_SIZE // MOBA_BLOCK
    small = pl.BlockSpec((1, tq, W), lambda b, s, pt: (b, 0, 0))

    def page_spec(j):
        return pl.BlockSpec((1, PAGE_SIZE, W), lambda b, s, pt: (pt[b, s * P + j], 0, 0))

    grid_spec = pltpu.PrefetchScalarGridSpec(
        num_scalar_prefetch=1,
        grid=(B, n_pages // P),
        in_specs=[small, small, small] + [page_spec(j) for j in range(P)] * 2,
        out_specs=pl.BlockSpec((1, hq, B_HEAD_DIM), lambda b, s, pt: (b, 0, 0)),
        scratch_shapes=[pltpu.VMEM((hq, W), F32), pltpu.VMEM((nbp, W), F32),
                        pltpu.VMEM((nbp, hq, B_HEAD_DIM), F32), pltpu.VMEM((nbp, hq, B_HEAD_DIM), F32),
                        pltpu.VMEM((nbp, hq, B_HEAD_DIM), F32)],
    )
    return pl.pallas_call(
        _moba_sample_kernel,
        grid_spec=grid_spec,
        out_shape=jax.ShapeDtypeStruct((B, hq, B_HEAD_DIM), F32),
        compiler_params=_cparams(("arbitrary", "arbitrary")),
        name="moba_sample",
    )(page_table, q, kn, vn, *([cache_k] * P), *([cache_v] * P))


def _mixout_kernel(x_ref, oa_ref, ob_ref, pg_ref, gt_ref, gp_ref, wa_ref, wb_ref, o_ref):
    gb, tm, d = x_ref.shape
    aw = oa_ref.shape[2]
    pg = pg_ref[...]
    ma = (oa_ref[...] * jax.nn.sigmoid(pg[..., :aw])).reshape(gb * tm, aw).astype(BF16)
    mb = (ob_ref[...] * jax.nn.sigmoid(pg[..., aw:])).reshape(gb * tm, pg.shape[2] - aw).astype(BF16)
    f = jnp.dot(ma, wa_ref[...], preferred_element_type=F32) + jnp.dot(mb, wb_ref[...], preferred_element_type=F32)
    y = f * lax.rsqrt(jnp.mean(f * f, axis=-1, keepdims=True) + RMS_EPS)
    y = y.reshape(gb, tm, d) * gp_ref[...]
    o_ref[...] = x_ref[...] + gt_ref[...] * y


def _mixout(x, oa, ob, pg, gt, gpost, wa, wb, gb, tm):
    G, T, D = x.shape
    aw, bw = oa.shape[2], ob.shape[2]
    row = lambda n: pl.BlockSpec((gb, tm, n), lambda a, i: (a, i, 0))
    return pl.pallas_call(
        _mixout_kernel,
        grid=(G // gb, T // tm),
        in_specs=[row(D), row(aw), row(bw), row(aw + bw),
                  pl.BlockSpec((gb, 1, D), lambda a, i: (a, 0, 0)),
                  pl.BlockSpec((1, 1, D), lambda a, i: (0, 0, 0)),
                  pl.BlockSpec(wa.shape, lambda a, i: (0, 0)),
                  pl.BlockSpec(wb.shape, lambda a, i: (0, 0))],
        out_specs=row(D),
        out_shape=jax.ShapeDtypeStruct((G, T, D), F32),
        compiler_params=_cparams(("arbitrary", "arbitrary")),
        name="mix_out",
    )(x, oa, ob, pg, gt, gpost, wa, wb)


def _ffn_kernel(x_ref, gpre_ref, sc_ref, sh_ref, gt_ref, gpost_ref, wg_ref, wu_ref, wo_ref, o_ref, h_ref, acc_ref):
    gb, tm, d = x_ref.shape
    j = pl.program_id(2)

    @pl.when(j == 0)
    def _():
        h = _norm_mod(x_ref[...], gpre_ref[...], sc_ref[...], sh_ref[...])
        h_ref[...] = h.reshape(gb * tm, d).astype(BF16)
        acc_ref[...] = jnp.zeros(acc_ref.shape, F32)

    h = h_ref[...]
    g = jnp.dot(h, wg_ref[...], preferred_element_type=F32)
    u = jnp.dot(h, wu_ref[...], preferred_element_type=F32)
    act = (g * jax.nn.sigmoid(g) * u).astype(BF16)
    acc_ref[...] += jnp.dot(act, wo_ref[...], preferred_element_type=F32)

    @pl.when(j == pl.num_programs(2) - 1)
    def _():
        f = acc_ref[...]
        y = f * lax.rsqrt(jnp.mean(f * f, axis=-1, keepdims=True) + RMS_EPS)
        y = y.reshape(gb, tm, d) * gpost_ref[...]
        o_ref[...] = x_ref[...] + gt_ref[...] * y


def _ffn(x, gpre, sc, sh, gt, gpost, w_in, w_out, gb, tm, tf):
    G, T, D = x.shape
    dff = w_out.shape[0]
    nj = dff // tf
    row = pl.BlockSpec((gb, tm, D), lambda a, i, j: (a, i, 0))
    mod = pl.BlockSpec((gb, 1, D), lambda a, i, j: (a, 0, 0))
    gain = pl.BlockSpec((1, 1, D), lambda a, i, j: (0, 0, 0))
    return pl.pallas_call(
        _ffn_kernel,
        grid=(G // gb, T // tm, nj),
        in_specs=[row, gain, mod, mod, mod, gain,
                  pl.BlockSpec((D, tf), lambda a, i, j: (0, j)),
                  pl.BlockSpec((D, tf), lambda a, i, j: (0, j + nj)),
                  pl.BlockSpec((tf, D), lambda a, i, j: (j, 0))],
        out_specs=row,
        out_shape=jax.ShapeDtypeStruct((G, T, D), F32),
        scratch_shapes=[pltpu.VMEM((gb * tm, D), BF16), pltpu.VMEM((gb * tm, D), F32)],
        compiler_params=_cparams(("arbitrary", "arbitrary", "arbitrary")),
        name="ffn",
    )(x, gpre, sc, sh, gt, gpost, w_in, w_in, w_out)


def _pair_state(s):
    B, H, n, _ = s.shape
    return s.reshape(B, H // 2, 2, n, n).transpose(0, 1, 3, 2, 4).reshape(B, (H // 2) * n, 2 * n)


def _unpair_state(s, H):
    B = s.shape[0]
    n = A_HEAD_DIM
    return s.reshape(B, H // 2, n, 2, n).transpose(0, 1, 3, 2, 4).reshape(B, H, n, n)


def _trunk(x, mod, shift0, wkv0, moba_fn, W, gb, tm):
    G, T, D = x.shape
    aw = W["w0"].shape[-1]
    bw = D - aw
    ac = W["mu"].shape[-1]
    m2 = W["m2"]
    md = [mod[:, c:c + 1, :] for c in range(6)]
    sh_m, sc_m, gt_m, sh_f, sc_f, gt_f = md

    proj = lambda w, tn: _norm_mod_matmul(x, W["g_pre_mix"], sc_m, sh_m, w, gb, tm, tn)
    pa = proj(W["w_in_a"], ac // 2)
    q = proj(W["w_in_q"], bw)
    k = proj(W["w_in_k"], bw)
    v = proj(W["w_in_v"], bw)
    pg = proj(W["w_in_g"], 1024)

    tp = min(T, 256)
    r_, w_, k_, v_, a_, b_, g_, bon, last = _rwkv_prep(
        pa, shift0[:, None, :], W["mu"], W["w0"], W["a0"], W["k_k"], W["k_a"], W["r_k"], W["wwa"], W["g2"], m2, tp)
    C = min(T, A_HEAD_DIM)
    npair = aw // LANES
    rs = lambda z: z.reshape(G, T, npair, LANES)
    yt, s_fin = _wkv_scan(rs(r_), rs(w_), rs(k_), rs(v_), rs(a_), rs(b_), _pair_state(wkv0), m2, C)
    o_a = _rwkv_post(yt, g_, bon, W["lnx_w"], W["lnx_b"], m2)

    o_b = moba_fn(q, k, v)

    x1 = _mixout(x, o_a, o_b, pg, gt_m, W["g_post_mix"], W["w_out_a"], W["w_out_b"], gb, min(tm, 256))
    y = _ffn(x1, W["g_pre_ffn"], sc_f, sh_f, gt_f, W["g_post_ffn"], W["w_ffn_in"], W["w_ffn_out"], gb, tm, 512)
    return y, last[:, 0, :], _unpair_state(s_fin, aw // A_HEAD_DIM), k, v


def kernel(x_prompt, x_sample, cache_k, cache_v, page_table, state_wkv, state_shift, c_prompt, c_sample, w_ada, b_ada, g_pre_mix, g_post_mix, g_pre_ffn, g_post_ffn, w_in, mu_shift, w0, w2, a0, a2, g2, k_k, k_a, r_k, lnx_w, lnx_b, w_out, w_ffn_in, w_ffn_out):
    depth = w_in.shape[0]
    assert depth == 1
    l = 0
    Bp, Tp, D = x_prompt.shape
    Bs, Ts, _ = x_sample.shape
    aw = w0.shape[-1]
    bw = D - aw
    ac = mu_shift.shape[-1]
    nh_a = aw // A_HEAD_DIM
    nh_b = bw // B_HEAD_DIM
    assert Bp == 1

    nc = Bp + Bs
    ncp = -(-nc // SUBLANES) * SUBLANES
    c_all = jnp.concatenate([c_prompt, c_sample, jnp.zeros((ncp - nc, D), F32)], axis=0)
    mod = _ada(c_all, w_ada[l], b_ada[l]).reshape(ncp, 6, D)
    mod_p, mod_s = mod[:Bp], mod[Bp:nc]

    wi = w_in[l]
    zeros_wa = jnp.zeros((DECAY_LORA, aw), F32)
    wwa = jnp.concatenate([jnp.concatenate([w2[l], zeros_wa], axis=1),
                           jnp.concatenate([zeros_wa, a2[l]], axis=1)], axis=0).astype(BF16)
    W = dict(
        m2=_block_ones(),
        g_pre_mix=g_pre_mix[l].reshape(1, 1, D), g_post_mix=g_post_mix[l].reshape(1, 1, D),
        g_pre_ffn=g_pre_ffn[l].reshape(1, 1, D), g_post_ffn=g_post_ffn[l].reshape(1, 1, D),
        w_in_a=wi[:, :ac].astype(BF16),
        w_in_q=wi[:, ac:ac + bw].astype(BF16),
        w_in_k=wi[:, ac + bw:ac + 2 * bw].astype(BF16),
        w_in_v=wi[:, ac + 2 * bw:ac + 3 * bw].astype(BF16),
        w_in_g=wi[:, ac + 3 * bw:].astype(BF16),
        mu=mu_shift[l].reshape(1, ac), w0=w0[l].reshape(1, aw), a0=a0[l].reshape(1, aw),
        k_k=k_k[l].reshape(1, aw), k_a=k_a[l].reshape(1, aw), r_k=r_k[l].reshape(1, aw),
        wwa=wwa, g2=g2[l].astype(BF16),
        lnx_w=lnx_w[l].reshape(1, aw), lnx_b=lnx_b[l].reshape(1, aw),
        w_out_a=w_out[l][:aw].astype(BF16), w_out_b=w_out[l][aw:].astype(BF16),
        w_ffn_in=w_ffn_in[l].astype(BF16), w_ffn_out=w_ffn_out[l].astype(BF16),
    )

    def moba_prompt_fn(q, k, v):
        kb, vt, km = _moba_prep(k[0], v[0])
        return _moba_prompt(q[0], kb, vt, km[:, 0, :])[None]

    n_pool = cache_k.shape[1]
    ck = cache_k[l].reshape(n_pool, PAGE_SIZE, bw)
    cv = cache_v[l].reshape(n_pool, PAGE_SIZE, bw)

    def moba_sample_fn(q, k, v):
        o = _moba_sample(q, k, v, ck, cv, page_table)
        return o.reshape(Bs, nh_b, Ts, B_HEAD_DIM).transpose(0, 2, 1, 3).reshape(Bs, Ts, bw)

    zero_shift = jnp.zeros((Bp, ac), F32)
    zero_wkv = jnp.zeros((Bp, nh_a, A_HEAD_DIM, A_HEAD_DIM), F32)
    yp, shp, wkvp, kp, vp = _trunk(x_prompt, mod_p, zero_shift, zero_wkv, moba_prompt_fn, W, 1, 512)
    ys, shs, wkvs, ksm, vsm = _trunk(x_sample, mod_s, state_shift[l], state_wkv[l], moba_sample_fn, W, Bs, Ts)

    hd = lambda z, B, T: z.reshape(1, B, T, nh_b, B_HEAD_DIM)
    return (yp, ys, hd(kp, Bp, Tp), hd(vp, Bp, Tp), wkvp[None], shp[None],
            hd(ksm, Bs, Ts), hd(vsm, Bs, Ts), wkvs[None], shs[None])
```

```python
import functools

import numpy as np
import jax
import jax.numpy as jnp
from jax import lax
from jax.experimental import pallas as pl
from jax.experimental.pallas import tpu as pltpu

F32 = jnp.float32
BF16 = jnp.bfloat16

A_HEAD_DIM = 64
DECAY_LORA = 64
AAA_LORA = 64
GATE_LORA = 128
B_HEAD_DIM = 128
MOBA_BLOCK = 256
MOBA_TOPK = 3
PAGE_SIZE = 128
LNX_EPS = 64e-5
RMS_EPS = 1e-6
NEG = -1e30

LANES = 128
SUBLANES = 8
VMEM_LIMIT = 56 * 1024 * 1024


def _cparams(sem):
    return pltpu.CompilerParams(dimension_semantics=sem, vmem_limit_bytes=VMEM_LIMIT)


def _split_bf16(x):
    hi = x.astype(BF16)
    lo = (x - hi.astype(F32)).astype(BF16)
    return hi, lo


def _seg_sum128(x, m2):
    hi, lo = _split_bf16(x)
    return jnp.dot(jnp.concatenate([hi, lo], axis=1), m2, preferred_element_type=F32)


def _seg_sum(x, m2):
    n = x.shape[1] // LANES
    return jnp.concatenate([_seg_sum128(x[:, c * LANES:(c + 1) * LANES], m2) for c in range(n)], axis=1)


def _block_ones_np():
    i = np.arange(LANES)
    return (i[:, None] // A_HEAD_DIM == i[None, :] // A_HEAD_DIM).astype(np.float32)


def _block_ones():
    m = _block_ones_np()
    return jnp.asarray(np.concatenate([m, m], axis=0), dtype=BF16)


def _ada_kernel(c_ref, w_ref, b_ref, o_ref):
    c = c_ref[...]
    s = (c * jax.nn.sigmoid(c)).astype(BF16)
    o_ref[...] = jnp.dot(s, w_ref[...].astype(BF16), preferred_element_type=F32) + b_ref[...]


def _ada(c, w_ada, b_ada):
    m, d = c.shape
    n = w_ada.shape[1]
    tn = 1024
    return pl.pallas_call(
        _ada_kernel,
        grid=(n // tn,),
        in_specs=[pl.BlockSpec((m, d), lambda j: (0, 0)),
                  pl.BlockSpec((d, tn), lambda j: (0, j)),
                  pl.BlockSpec((1, tn), lambda j: (0, j))],
        out_specs=pl.BlockSpec((m, tn), lambda j: (0, j)),
        out_shape=jax.ShapeDtypeStruct((m, n), F32),
        compiler_params=_cparams(("arbitrary",)),
        name="ada_mod",
    )(c, w_ada, b_ada.reshape(1, n))


def _norm_mod(x, g, sc, sh):
    ms = jnp.mean(x * x, axis=-1, keepdims=True)
    y = x * lax.rsqrt(ms + RMS_EPS) * g
    return y * (1.0 + sc) + sh


def _nmm_kernel(x_ref, g_ref, sc_ref, sh_ref, w_ref, o_ref, h_ref):
    gb, tm, d = x_ref.shape

    @pl.when(pl.program_id(2) == 0)
    def _():
        h = _norm_mod(x_ref[...], g_ref[...], sc_ref[...], sh_ref[...])
        h_ref[...] = h.reshape(gb * tm, d).astype(BF16)

    o = jnp.dot(h_ref[...], w_ref[...], preferred_element_type=F32)
    o_ref[...] = o.reshape(o_ref.shape).astype(o_ref.dtype)


def _norm_mod_matmul(x, g, sc, sh, w, gb, tm, tn, out_dtype=F32):
    G, T, D = x.shape
    N = w.shape[1]
    return pl.pallas_call(
        _nmm_kernel,
        grid=(G // gb, T // tm, N // tn),
        in_specs=[pl.BlockSpec((gb, tm, D), lambda a, i, j: (a, i, 0)),
                  pl.BlockSpec((1, 1, D), lambda a, i, j: (0, 0, 0)),
                  pl.BlockSpec((gb, 1, D), lambda a, i, j: (a, 0, 0)),
                  pl.BlockSpec((gb, 1, D), lambda a, i, j: (a, 0, 0)),
                  pl.BlockSpec((D, tn), lambda a, i, j: (0, j))],
        out_specs=pl.BlockSpec((gb, tm, tn), lambda a, i, j: (a, i, j)),
        out_shape=jax.ShapeDtypeStruct((G, T, N), out_dtype),
        scratch_shapes=[pltpu.VMEM((gb * tm, D), BF16)],
        compiler_params=_cparams(("arbitrary", "arbitrary", "arbitrary")),
        name="norm_mod_matmul",
    )(x, g, sc, sh, w)


def _softplus(z):
    return jnp.maximum(z, 0.0) + jnp.log(1.0 + jnp.exp(-jnp.abs(z)))


def _prep_kernel(pa_ref, sh0_ref, mu_ref, w0_ref, a0_ref, kk_ref, ka_ref, rk_ref, wwa_ref, g2_ref, m2_ref,
                 r_o, w_o, k_o, v_o, a_o, b_o, g_o, bon_o, last_o, carry_ref):
    tm = pa_ref.shape[1]
    aw = r_o.shape[2]

    @pl.when(pl.program_id(1) == 0)
    def _():
        carry_ref[...] = sh0_ref[0]

    pa = pa_ref[0]
    rolled = pltpu.roll(pa, 1, axis=0)
    row = lax.broadcasted_iota(jnp.int32, (tm, 1), 0)
    prev = jnp.where(row == 0, carry_ref[...], rolled)
    last = pa[tm - 1:tm, :]
    carry_ref[...] = last
    last_o[0] = last

    ps = pa + (prev - pa) * mu_ref[...]
    r = ps[:, 0:aw]
    k = ps[:, aw:2 * aw]
    v = ps[:, 2 * aw:3 * aw]
    o3 = 3 * aw
    x_wa = ps[:, o3:o3 + DECAY_LORA + AAA_LORA]
    gl = ps[:, o3 + DECAY_LORA + AAA_LORA:]
    lane = lax.broadcasted_iota(jnp.int32, x_wa.shape, 1)
    z = jnp.where(lane < DECAY_LORA, jnp.tanh(x_wa), x_wa).astype(BF16)
    wa = jnp.dot(z, wwa_ref[...], preferred_element_type=F32)
    lw = wa[:, :aw]
    la = wa[:, aw:]
    g = jnp.dot(jax.nn.sigmoid(gl).astype(BF16), g2_ref[...], preferred_element_type=F32)

    w_log = -_softplus(-(w0_ref[...] + lw)) - 0.5
    decay = jnp.exp(-jnp.exp(w_log))
    a = jax.nn.sigmoid(a0_ref[...] + la)
    m2 = m2_ref[...]
    kk = k * kk_ref[...]
    kk = kk * lax.rsqrt(jnp.maximum(_seg_sum(kk * kk, m2), 1e-24))
    k_mod = k * (1.0 + (a - 1.0) * ka_ref[...])
    bonus = _seg_sum(r * k_mod * rk_ref[...], m2) * v

    r_o[0] = r
    w_o[0] = decay
    k_o[0] = k_mod
    v_o[0] = v
    a_o[0] = -kk
    b_o[0] = kk * a
    g_o[0] = g
    bon_o[0] = bonus


def _rwkv_prep(pa, shift0, mu, w0, a0, k_k, k_a, r_k, wwa, g2, m2, tm):
    G, T, AC = pa.shape
    aw = w0.shape[-1]
    vec = lambda n: pl.BlockSpec((1, n), lambda a, i: (0, 0))
    full = lambda s: pl.BlockSpec(s, lambda a, i: (0, 0))
    big = pl.BlockSpec((1, tm, aw), lambda a, i: (a, i, 0))
    outs = [jax.ShapeDtypeStruct((G, T, aw), F32)] * 8 + [jax.ShapeDtypeStruct((G, 1, AC), F32)]
    return pl.pallas_call(
        _prep_kernel,
        grid=(G, T // tm),
        in_specs=[pl.BlockSpec((1, tm, AC), lambda a, i: (a, i, 0)),
                  pl.BlockSpec((1, 1, AC), lambda a, i: (a, 0, 0)),
                  vec(AC), vec(aw), vec(aw), vec(aw), vec(aw), vec(aw),
                  full(wwa.shape), full(g2.shape), full(m2.shape)],
        out_specs=[big] * 8 + [pl.BlockSpec((1, 1, AC), lambda a, i: (a, 0, 0))],
        out_shape=outs,
        scratch_shapes=[pltpu.VMEM((1, AC), F32)],
        compiler_params=_cparams(("arbitrary", "arbitrary")),
        name="rwkv_prep",
    )(pa, shift0, mu, w0, a0, k_k, k_a, r_k, wwa, g2, m2)


SCAN_ROW_GROUPS = 1


def _half_lane_unzip(xt):
    lo_half = lax.broadcasted_iota(jnp.int32, (A_HEAD_DIM, LANES), 1) < A_HEAD_DIM
    top = xt[0:A_HEAD_DIM]
    bot = xt[A_HEAD_DIM:2 * A_HEAD_DIM]
    return (jnp.where(lo_half, top, pltpu.roll(bot, A_HEAD_DIM, axis=1)),
            jnp.where(lo_half, pltpu.roll(top, A_HEAD_DIM, axis=1), bot))


def _scan_kernel(r_ref, w_ref, k_ref, a_ref, b_ref, v2_ref, s0_ref, my_ref, e_ref, y_ref, sf_ref, s_ref, vt_ref):
    C = r_ref.shape[1]
    npair = r_ref.shape[2]
    rows = npair * A_HEAD_DIM

    @pl.when(pl.program_id(1) == 0)
    def _():
        s_ref[...] = s0_ref[0]

    v2 = v2_ref[0]
    if C < A_HEAD_DIM:
        v2 = jnp.concatenate([v2, jnp.zeros((A_HEAD_DIM - C, v2.shape[1]), F32)], axis=0)
    for pp in range(npair // 2):
        x = jnp.concatenate([v2[:, (2 * pp) * LANES:(2 * pp + 1) * LANES],
                             v2[:, (2 * pp + 1) * LANES:(2 * pp + 2) * LANES]], axis=0)
        for j, o in enumerate(_half_lane_unzip(x.T)):
            hi, lo = _split_bf16(o)
            p = 2 * pp + j
            vt_ref[p * A_HEAD_DIM:(p + 1) * A_HEAD_DIM, :] = jnp.concatenate([hi, lo], axis=1)

    y_ref[...] = jnp.zeros(y_ref.shape, F32)
    my = my_ref[...]

    ngrp = SCAN_ROW_GROUPS
    ppg = npair // ngrp
    grows = ppg * A_HEAD_DIM
    tlane = lax.broadcasted_iota(jnp.int32, (grows, LANES), 1) % A_HEAD_DIM

    def bc(ref, t, g):
        return jnp.concatenate(
            [jnp.broadcast_to(ref[0, t, p:p + 1, :], (A_HEAD_DIM, LANES)) for p in range(g * ppg, (g + 1) * ppg)],
            axis=0)

    def seg_sums(S, t_a, t_r, g):
        lhs = jnp.concatenate([(S * bc(a_ref, t_a, g)).astype(BF16), (S * bc(r_ref, t_r, g)).astype(BF16)], axis=1)
        out = jnp.dot(lhs, my, preferred_element_type=F32)
        return out[:, :LANES], out[:, LANES:]

    def put_y(yb, t, g):
        sl = slice(g * grows, (g + 1) * grows)
        y_ref[0, 0, sl, :] = jnp.where(tlane == t, yb, y_ref[0, 0, sl, :])

    def two_steps(j, carry):
        t0 = 2 * j
        t1 = t0 + 1
        for g in range(ngrp):
            sl = slice(g * grows, (g + 1) * grows)
            vb = jnp.dot(vt_ref[sl, :], e_ref[j], preferred_element_type=F32)
            S = s_ref[sl, :]
            for t, c in ((t0, 0), (t1, 1)):
                sa, y_prev = seg_sums(S, t, jnp.maximum(t - 1, 0), g)
                put_y(y_prev, t - 1, g)
                S = (S * bc(w_ref, t, g) + sa * bc(b_ref, t, g)
                     + vb[:, c * LANES:(c + 1) * LANES] * bc(k_ref, t, g))
            s_ref[sl, :] = S
        return carry

    lax.fori_loop(0, C // 2, two_steps, 0)
    for g in range(ngrp):
        S = s_ref[g * grows:(g + 1) * grows, :]
        put_y(seg_sums(S, C - 1, C - 1, g)[1], C - 1, g)
    sf_ref[0] = s_ref[...]


def _onehot_columns():
    j = np.arange(A_HEAD_DIM // 2)[:, None, None, None]
    i = np.arange(2 * LANES)[None, :, None, None] % LANES
    c = np.arange(2)[None, None, :, None]
    l = np.arange(LANES)[None, None, None, :]
    e = i == (l // A_HEAD_DIM) * A_HEAD_DIM + 2 * j + c
    return jnp.asarray(e.reshape(A_HEAD_DIM // 2, 2 * LANES, 2 * LANES), dtype=BF16)


def _block_ones_pair():
    m = _block_ones_np()
    z = np.zeros_like(m)
    return jnp.asarray(np.block([[m, z], [z, m]]), dtype=BF16)


def _wkv_scan(r, w, k, a, b, v2, s0, C):
    G, T, npair, _ = r.shape
    rows = npair * A_HEAD_DIM
    assert C % 2 == 0
    e = _onehot_columns()[:C // 2]
    my = _block_ones_pair()
    seq = pl.BlockSpec((1, C, npair, LANES), lambda g, c: (g, c, 0, 0))
    st = pl.BlockSpec((1, rows, LANES), lambda g, c: (g, 0, 0))
    return pl.pallas_call(
        _scan_kernel,
        grid=(G, T // C),
        in_specs=[seq] * 5 + [pl.BlockSpec((1, C, v2.shape[2]), lambda g, c: (g, c, 0)), st,
                              pl.BlockSpec(my.shape, lambda g, c: (0, 0)),
                              pl.BlockSpec(e.shape, lambda g, c: (0, 0, 0))],
        out_specs=[pl.BlockSpec((1, 1, rows, LANES), lambda g, c: (g, c, 0, 0)), st],
        out_shape=[jax.ShapeDtypeStruct((G, T // C, rows, LANES), F32),
                   jax.ShapeDtypeStruct((G, rows, LANES), F32)],
        scratch_shapes=[pltpu.VMEM((rows, LANES), F32), pltpu.VMEM((rows, 2 * LANES), BF16)],
        compiler_params=_cparams(("arbitrary", "arbitrary")),
        name="wkv_scan",
    )(r, w, k, a, b, v2, s0, my, e)


def _post_kernel(yt_ref, g_ref, bon_ref, lw_ref, lb_ref, m2_ref, o_ref):
    C = o_ref.shape[1]
    rows = yt_ref.shape[2]
    chunks = []
    for pp in range(rows // LANES):
        rt = yt_ref[0, 0, pp * LANES:(pp + 1) * LANES, :].T
        chunks.extend(o[:C] for o in _half_lane_unzip(rt))
    y = jnp.concatenate(chunks, axis=1)
    m2 = m2_ref[...]
    inv = 1.0 / A_HEAD_DIM
    mean = _seg_sum(y, m2) * inv
    yc = y - mean
    var = _seg_sum(yc * yc, m2) * inv
    yn = yc * lax.rsqrt(var + LNX_EPS) * lw_ref[...] + lb_ref[...]
    o_ref[0] = ((yn + bon_ref[0]) * g_ref[0]).astype(o_ref.dtype)


def _rwkv_post(yt, g, bon, lnx_w, lnx_b, m2):
    G, NC, rows, _ = yt.shape
    T, aw = g.shape[1], g.shape[2]
    C = T // NC
    big = pl.BlockSpec((1, C, aw), lambda a, c: (a, c, 0))
    vec = pl.BlockSpec((1, aw), lambda a, c: (0, 0))
    return pl.pallas_call(
        _post_kernel,
        grid=(G, NC),
        in_specs=[pl.BlockSpec((1, 1, rows, LANES), lambda a, c: (a, c, 0, 0)), big, big, vec, vec,
                  pl.BlockSpec(m2.shape, lambda a, c: (0, 0))],
        out_specs=big,
        out_shape=jax.ShapeDtypeStruct((G, T, aw), F32),
        compiler_params=_cparams(("arbitrary", "arbitrary")),
        name="rwkv_post",
    )(yt, g, bon, lnx_w, lnx_b, m2)


def _moba_prep_kernel(k_ref, v_ref, kb_ref, vt_ref, km_ref):
    k = k_ref[...]
    kb_ref[...] = k.astype(BF16)
    km_ref[0] = jnp.mean(k, axis=0, keepdims=True)
    vt_ref[...] = v_ref[...].T.astype(BF16)


def _moba_prep(k, v):
    T, W = k.shape
    nb = T // MOBA_BLOCK
    blk = pl.BlockSpec((MOBA_BLOCK, W), lambda n: (n, 0))
    return pl.pallas_call(
        _moba_prep_kernel,
        grid=(nb,),
        in_specs=[blk, blk],
        out_specs=[blk, pl.BlockSpec((W, MOBA_BLOCK), lambda n: (0, n)),
                   pl.BlockSpec((1, 1, W), lambda n: (n, 0, 0))],
        out_shape=[jax.ShapeDtypeStruct((T, W), BF16), jax.ShapeDtypeStruct((W, T), BF16),
                   jax.ShapeDtypeStruct((nb, 1, W), F32)],
        compiler_params=_cparams(("arbitrary",)),
        name="moba_prep",
    )(k, v)


def _select_topk(gate, valid, axis):
    idx = lax.broadcasted_iota(jnp.int32, gate.shape, axis)
    g = jnp.where(valid, gate, NEG)
    sel = jnp.zeros(gate.shape, jnp.bool_)
    for _ in range(MOBA_TOPK):
        m = jnp.max(g, axis=axis, keepdims=True)
        first = jnp.min(jnp.where(g == m, idx, jnp.int32(2 ** 30)), axis=axis, keepdims=True)
        pick = idx == first
        sel = jnp.logical_or(sel, pick)
        g = jnp.where(pick, -jnp.inf, g)
    return jnp.logical_and(sel, valid)


PROMPT_HEADS_PER_STEP = 2
PROMPT_BLOCKS_PER_ITER = 2
LOG2E = 1.4426950408889634


def _moba_prompt_kernel(q_ref, k_ref, vt_ref, km_ref, o_ref, sel_ref):
    i = pl.program_id(1)
    tq = q_ref.shape[0]
    nb = km_ref.shape[0]
    dh = B_HEAD_DIM
    nt = (((1,), (1,)), ((), ()))
    kt = PROMPT_BLOCKS_PER_ITER * MOBA_BLOCK
    heads = range(PROMPT_HEADS_PER_STEP)
    blk = lax.broadcasted_iota(jnp.int32, (nb, tq), 0)
    key = lax.broadcasted_iota(jnp.int32, (MOBA_BLOCK, tq), 0)
    qi = lax.broadcasted_iota(jnp.int32, (MOBA_BLOCK, tq), 1)
    start = pl.multiple_of(i * MOBA_BLOCK, MOBA_BLOCK)

    qbs, init = [], []
    for u in heads:
        sl = slice(u * dh, (u + 1) * dh)
        q = q_ref[:, sl]
        qb = (q * (dh ** -0.5 * LOG2E)).astype(BF16)
        qbs.append(qb)
        gate_t = lax.dot_general(km_ref[:, sl], q, nt, precision=lax.Precision.HIGHEST,
                                 preferred_element_type=F32)
        sel_ref[u] = _select_topk(gate_t, blk < i, 0).astype(F32)
        s_t = lax.dot_general(k_ref[pl.ds(start, MOBA_BLOCK), sl], qb, nt, preferred_element_type=F32)
        s_t = jnp.where(key <= qi, s_t, NEG)
        m0 = jnp.max(s_t, axis=0, keepdims=True)
        p = jnp.exp2(s_t - m0)
        l0 = jnp.sum(p, axis=0, keepdims=True)
        acc0 = jnp.dot(vt_ref[sl, pl.ds(start, MOBA_BLOCK)], p.astype(BF16), preferred_element_type=F32)
        init.append((m0, l0, acc0))

    n_groups = nb // PROMPT_BLOCKS_PER_ITER

    def scores(g, u):
        st = pl.multiple_of(g * kt, kt)
        return lax.dot_general(k_ref[pl.ds(st, kt), u * dh:(u + 1) * dh], qbs[u], nt, preferred_element_type=F32)

    def body(g, carry):
        out = []
        for u in heads:
            m, l, acc, s_t, pv = carry[u]
            s_next = scores(jnp.minimum(g + 1, n_groups - 1), u)
            acc = acc + pv
            parts = []
            for c in range(PROMPT_BLOCKS_PER_ITER):
                ok = sel_ref[u, pl.ds(g * PROMPT_BLOCKS_PER_ITER + c, 1), :] > 0.5
                parts.append(jnp.where(ok, s_t[c * MOBA_BLOCK:(c + 1) * MOBA_BLOCK], NEG))
            s_t = jnp.concatenate(parts, axis=0)
            m_new = jnp.maximum(m, jnp.max(s_t, axis=0, keepdims=True))
            alpha = jnp.exp2(m - m_new)
            p = jnp.exp2(s_t - m_new)
            l = alpha * l + jnp.sum(p, axis=0, keepdims=True)
            st = pl.multiple_of(g * kt, kt)
            pv = jnp.dot(vt_ref[u * dh:(u + 1) * dh, pl.ds(st, kt)], p.astype(BF16), preferred_element_type=F32)
            out.append((m_new, l, alpha * acc, s_next, pv))
        return tuple(out)

    n_iter = (i + PROMPT_BLOCKS_PER_ITER - 1) // PROMPT_BLOCKS_PER_ITER
    start_carry = tuple((m0, l0, acc0, scores(0, u), jnp.zeros((dh, tq), F32))
                        for u, (m0, l0, acc0) in zip(heads, init))
    fin = lax.fori_loop(0, n_iter, body, start_carry)
    for u in heads:
        m, l, acc, _, pv = fin[u]
        o_ref[:, u * dh:(u + 1) * dh] = ((acc + pv) / l).T


def _moba_prompt(q, kb, vt, kmean):
    T, W = q.shape
    hw = PROMPT_HEADS_PER_STEP * B_HEAD_DIM
    nb = T // MOBA_BLOCK
    assert nb % PROMPT_BLOCKS_PER_ITER == 0
    return pl.pallas_call(
        _moba_prompt_kernel,
        grid=(W // hw, nb),
        in_specs=[pl.BlockSpec((MOBA_BLOCK, hw), lambda h, i: (i, h)),
                  pl.BlockSpec((T, hw), lambda h, i: (0, h)),
                  pl.BlockSpec((hw, T), lambda h, i: (h, 0)),
                  pl.BlockSpec((nb, hw), lambda h, i: (0, h))],
        out_specs=pl.BlockSpec((MOBA_BLOCK, hw), lambda h, i: (i, h)),
        out_shape=jax.ShapeDtypeStruct((T, W), F32),
        scratch_shapes=[pltpu.VMEM((PROMPT_HEADS_PER_STEP, nb, MOBA_BLOCK), F32)],
        compiler_params=_cparams(("arbitrary", "arbitrary")),
        name="moba_prompt",
    )(q, kb, vt, kmean)


SAMPLE_PAGES_PER_STEP = 8


def _moba_sample_kernel(pt_ref, q_ref, kn_ref, vn_ref, *rest):
    P = SAMPLE_PAGES_PER_STEP
    kp = rest[:P]
    vp = rest[P:2 * P]
    o_ref = rest[2 * P]
    km_ref, m_ref, l_ref, oacc_ref = rest[2 * P + 1:]
    s = pl.program_id(1)
    ns = pl.num_programs(1)
    tq, W = q_ref.shape[1], q_ref.shape[2]
    dh = B_HEAD_DIM
    nh = W // dh
    hq = nh * tq
    nbp = km_ref.shape[1]
    ppb = MOBA_BLOCK // PAGE_SIZE
    bps = P // ppb
    nt = (((1,), (1,)), ((), ()))

    def by_head(x):
        return jnp.concatenate([x[:, h * dh:(h + 1) * dh] for h in range(nh)], axis=0)

    qall = by_head(q_ref[0])
    qall_b = (qall * (dh ** -0.5)).astype(BF16)

    def partial(k_b, v_b, mask):
        sc = jnp.where(mask, lax.dot_general(qall_b, k_b, nt, preferred_element_type=F32), NEG)
        m = jnp.max(sc, axis=1, keepdims=True)
        p = jnp.exp(sc - m)
        l = jnp.sum(p, axis=1, keepdims=True)
        o = jnp.dot(p.astype(BF16), v_b, preferred_element_type=F32)
        return jnp.broadcast_to(m, (hq, dh)), jnp.broadcast_to(l, (hq, dh)), o

    nflat = MOBA_BLOCK * nh
    rowh = lax.broadcasted_iota(jnp.int32, (hq, nflat), 0) // tq
    colh = lax.broadcasted_iota(jnp.int32, (hq, nflat), 1) % nh
    same_head = rowh == colh
    for j in range(bps):
        kpages = [kp[j * ppb + u][0, 0] for u in range(ppb)]
        vpages = [vp[j * ppb + u][0, 0] for u in range(ppb)]
        n = s * bps + j
        kmean = sum(jnp.sum(pg, axis=0) for pg in kpages) * (1.0 / MOBA_BLOCK)
        for h in range(nh):
            km_ref[h, pl.ds(n, 1), :] = kmean[h:h + 1, :]
        kflat = jnp.concatenate([pg.reshape(PAGE_SIZE * nh, dh) for pg in kpages], axis=0).astype(BF16)
        vflat = jnp.concatenate([pg.reshape(PAGE_SIZE * nh, dh) for pg in vpages], axis=0).astype(BF16)
        m, l, o = partial(kflat, vflat, same_head)
        m_ref[n] = m
        l_ref[n] = l
        oacc_ref[n] = o

    @pl.when(s == ns - 1)
    def _():
        pad = jnp.zeros((LANES - hq, dh), F32)
        kn = jnp.concatenate([by_head(kn_ref[0]), pad], axis=0).astype(BF16)
        vn = jnp.concatenate([by_head(vn_ref[0]), pad], axis=0).astype(BF16)
        col = lax.broadcasted_iota(jnp.int32, (hq, LANES), 1)
        row = lax.broadcasted_iota(jnp.int32, (hq, LANES), 0)
        own_mask = jnp.logical_and(col // tq == row // tq, col % tq <= row % tq)
        m_own, l_own, o_own = partial(kn, vn, own_mask)

        gate = jnp.concatenate(
            [lax.dot_general(qall[h * tq:(h + 1) * tq], km_ref[h], nt, precision=lax.Precision.HIGHEST,
                             preferred_element_type=F32) for h in range(nh)], axis=0)
        sel = _select_topk(gate, jnp.ones(gate.shape, jnp.bool_), 1).astype(F32)
        mx = m_own
        for n in range(nbp):
            sn = sel[:, n:n + 1] > 0.5
            mx = jnp.where(sn, jnp.maximum(mx, m_ref[n]), mx)
        w_own = jnp.exp(m_own - mx)
        lsum = w_own * l_own
        osum = w_own * o_own
        for n in range(nbp):
            sn = sel[:, n:n + 1] > 0.5
            wn = jnp.where(sn, jnp.exp(m_ref[n] - mx), 0.0)
            lsum = lsum + wn * l_ref[n]
            osum = osum + wn * oacc_ref[n]
        o_ref[0] = osum / lsum


def _moba_sample(q, kn, vn, cache_k, cache_v, page_table):
    B, tq, W = q.shape
    nh = W // B_HEAD_DIM
    hq = nh * tq
    assert hq <= LANES
    n_pages = page_table.shape[1]
    P = SAMPLE_PAGES_PER_STEP
    nbp = n_pages * PAGE_SIZE // MOBA_BLOCK
    small = pl.BlockSpec((1, tq, W), lambda b, s, pt: (b, 0, 0))

    def page_spec(j):
        return pl.BlockSpec((1, 1, PAGE_SIZE, nh, B_HEAD_DIM), lambda b, s, pt: (0, pt[b, s * P + j], 0, 0, 0))

    grid_spec = pltpu.PrefetchScalarGridSpec(
        num_scalar_prefetch=1,
        grid=(B, n_pages // P),
        in_specs=[small, small, small] + [page_spec(j) for j in range(P)] * 2,
        out_specs=pl.BlockSpec((1, hq, B_HEAD_DIM), lambda b, s, pt: (b, 0, 0)),
        scratch_shapes=[pltpu.VMEM((nh, nbp, B_HEAD_DIM), F32),
                        pltpu.VMEM((nbp, hq, B_HEAD_DIM), F32), pltpu.VMEM((nbp, hq, B_HEAD_DIM), F32),
                        pltpu.VMEM((nbp, hq, B_HEAD_DIM), F32)],
    )
    return pl.pallas_call(
        _moba_sample_kernel,
        grid_spec=grid_spec,
        out_shape=jax.ShapeDtypeStruct((B, hq, B_HEAD_DIM), F32),
        compiler_params=_cparams(("arbitrary", "arbitrary")),
        name="moba_sample",
    )(page_table, q, kn, vn, *([cache_k] * P), *([cache_v] * P))


def _mixout_kernel(x_ref, oa_ref, ob_ref, pg_ref, gt_ref, gp_ref, wa_ref, wb_ref, o_ref):
    gb, tm, d = x_ref.shape
    aw = oa_ref.shape[2]
    pg = pg_ref[...]
    ma = (oa_ref[...] * jax.nn.sigmoid(pg[..., :aw])).reshape(gb * tm, aw).astype(BF16)
    mb = (ob_ref[...] * jax.nn.sigmoid(pg[..., aw:])).reshape(gb * tm, pg.shape[2] - aw).astype(BF16)
    f = jnp.dot(ma, wa_ref[...], preferred_element_type=F32) + jnp.dot(mb, wb_ref[...], preferred_element_type=F32)
    y = f * lax.rsqrt(jnp.mean(f * f, axis=-1, keepdims=True) + RMS_EPS)
    y = y.reshape(gb, tm, d) * gp_ref[...]
    o_ref[...] = x_ref[...] + gt_ref[...] * y


def _mixout(x, oa, ob, pg, gt, gpost, wa, wb, gb, tm):
    G, T, D = x.shape
    aw, bw = oa.shape[2], ob.shape[2]
    row = lambda n: pl.BlockSpec((gb, tm, n), lambda a, i: (a, i, 0))
    return pl.pallas_call(
        _mixout_kernel,
        grid=(G // gb, T // tm),
        in_specs=[row(D), row(aw), row(bw), row(aw + bw),
                  pl.BlockSpec((gb, 1, D), lambda a, i: (a, 0, 0)),
                  pl.BlockSpec((1, 1, D), lambda a, i: (0, 0, 0)),
                  pl.BlockSpec(wa.shape, lambda a, i: (0, 0)),
                  pl.BlockSpec(wb.shape, lambda a, i: (0, 0))],
        out_specs=row(D),
        out_shape=jax.ShapeDtypeStruct((G, T, D), F32),
        compiler_params=_cparams(("arbitrary", "arbitrary")),
        name="mix_out",
    )(x, oa, ob, pg, gt, gpost, wa, wb)


def _ffn_kernel(x_ref, gpre_ref, sc_ref, sh_ref, gt_ref, gpost_ref, wg_ref, wu_ref, wo_ref, o_ref, h_ref, acc_ref):
    gb, tm, d = x_ref.shape
    j = pl.program_id(2)

    @pl.when(j == 0)
    def _():
        h = _norm_mod(x_ref[...], gpre_ref[...], sc_ref[...], sh_ref[...])
        h_ref[...] = h.reshape(gb * tm, d).astype(BF16)
        acc_ref[...] = jnp.zeros(acc_ref.shape, F32)

    h = h_ref[...]
    g = jnp.dot(h, wg_ref[...], preferred_element_type=F32)
    u = jnp.dot(h, wu_ref[...], preferred_element_type=F32)
    act = (g * jax.nn.sigmoid(g) * u).astype(BF16)
    acc_ref[...] += jnp.dot(act, wo_ref[...], preferred_element_type=F32)

    @pl.when(j == pl.num_programs(2) - 1)
    def _():
        f = acc_ref[...]
        y = f * lax.rsqrt(jnp.mean(f * f, axis=-1, keepdims=True) + RMS_EPS)
        y = y.reshape(gb, tm, d) * gpost_ref[...]
        o_ref[...] = x_ref[...] + gt_ref[...] * y


def _ffn(x, gpre, sc, sh, gt, gpost, w_in, w_out, gb, tm, tf):
    G, T, D = x.shape
    dff = w_out.shape[0]
    nj = dff // tf
    row = pl.BlockSpec((gb, tm, D), lambda a, i, j: (a, i, 0))
    mod = pl.BlockSpec((gb, 1, D), lambda a, i, j: (a, 0, 0))
    gain = pl.BlockSpec((1, 1, D), lambda a, i, j: (0, 0, 0))
    return pl.pallas_call(
        _ffn_kernel,
        grid=(G // gb, T // tm, nj),
        in_specs=[row, gain, mod, mod, mod, gain,
                  pl.BlockSpec((D, tf), lambda a, i, j: (0, j)),
                  pl.BlockSpec((D, tf), lambda a, i, j: (0, j + nj)),
                  pl.BlockSpec((tf, D), lambda a, i, j: (j, 0))],
        out_specs=row,
        out_shape=jax.ShapeDtypeStruct((G, T, D), F32),
        scratch_shapes=[pltpu.VMEM((gb * tm, D), BF16), pltpu.VMEM((gb * tm, D), F32)],
        compiler_params=_cparams(("arbitrary", "arbitrary", "arbitrary")),
        name="ffn",
    )(x, gpre, sc, sh, gt, gpost, w_in, w_in, w_out)


def _pair_state(s):
    B, H, n, _ = s.shape
    return s.reshape(B, H // 2, 2, n, n).transpose(0, 1, 3, 2, 4).reshape(B, (H // 2) * n, 2 * n)


def _unpair_state(s, H):
    B = s.shape[0]
    n = A_HEAD_DIM
    return s.reshape(B, H // 2, n, 2, n).transpose(0, 1, 3, 2, 4).reshape(B, H, n, n)


def _trunk(x, mod, shift0, wkv0, moba_fn, W, gb, tm):
    G, T, D = x.shape
    aw = W["w0"].shape[-1]
    bw = D - aw
    ac = W["mu"].shape[-1]
    m2 = W["m2"]
    md = [mod[:, c:c + 1, :] for c in range(6)]
    sh_m, sc_m, gt_m, sh_f, sc_f, gt_f = md

    proj = lambda w, tn: _norm_mod_matmul(x, W["g_pre_mix"], sc_m, sh_m, w, gb, tm, tn)
    pa = proj(W["w_in_a"], ac // 2)
    q = proj(W["w_in_q"], bw)
    k = proj(W["w_in_k"], bw)
    v = proj(W["w_in_v"], bw)
    pg = proj(W["w_in_g"], 1024)

    tp = min(T, 256)
    r_, w_, k_, v_, a_, b_, g_, bon, last = _rwkv_prep(
        pa, shift0[:, None, :], W["mu"], W["w0"], W["a0"], W["k_k"], W["k_a"], W["r_k"], W["wwa"], W["g2"], m2, tp)
    C = min(T, A_HEAD_DIM)
    npair = aw // LANES
    rs = lambda z: z.reshape(G, T, npair, LANES)
    yt, s_fin = _wkv_scan(rs(r_), rs(w_), rs(k_), rs(a_), rs(b_), v_, _pair_state(wkv0), C)
    o_a = _rwkv_post(yt, g_, bon, W["lnx_w"], W["lnx_b"], m2)

    o_b = moba_fn(q, k, v)

    x1 = _mixout(x, o_a, o_b, pg, gt_m, W["g_post_mix"], W["w_out_a"], W["w_out_b"], gb, min(tm, 256))
    y = _ffn(x1, W["g_pre_ffn"], sc_f, sh_f, gt_f, W["g_post_ffn"], W["w_ffn_in"], W["w_ffn_out"], gb, tm, 512)
    return y, last[:, 0, :], _unpair_state(s_fin, aw // A_HEAD_DIM), k, v


def kernel(x_prompt, x_sample, cache_k, cache_v, page_table, state_wkv, state_shift, c_prompt, c_sample, w_ada, b_ada, g_pre_mix, g_post_mix, g_pre_ffn, g_post_ffn, w_in, mu_shift, w0, w2, a0, a2, g2, k_k, k_a, r_k, lnx_w, lnx_b, w_out, w_ffn_in, w_ffn_out):
    depth = w_in.shape[0]
    assert depth == 1
    l = 0
    Bp, Tp, D = x_prompt.shape
    Bs, Ts, _ = x_sample.shape
    aw = w0.shape[-1]
    bw = D - aw
    ac = mu_shift.shape[-1]
    nh_a = aw // A_HEAD_DIM
    nh_b = bw // B_HEAD_DIM
    assert Bp == 1

    nc = Bp + Bs
    ncp = -(-nc // SUBLANES) * SUBLANES
    c_all = jnp.concatenate([c_prompt, c_sample, jnp.zeros((ncp - nc, D), F32)], axis=0)
    mod = _ada(c_all, w_ada[l], b_ada[l]).reshape(ncp, 6, D)
    mod_p, mod_s = mod[:Bp], mod[Bp:nc]

    wi = w_in[l]
    zeros_wa = jnp.zeros((DECAY_LORA, aw), F32)
    wwa = jnp.concatenate([jnp.concatenate([w2[l], zeros_wa], axis=1),
                           jnp.concatenate([zeros_wa, a2[l]], axis=1)], axis=0).astype(BF16)
    W = dict(
        m2=_block_ones(),
        g_pre_mix=g_pre_mix[l].reshape(1, 1, D), g_post_mix=g_post_mix[l].reshape(1, 1, D),
        g_pre_ffn=g_pre_ffn[l].reshape(1, 1, D), g_post_ffn=g_post_ffn[l].reshape(1, 1, D),
        w_in_a=wi[:, :ac].astype(BF16),
        w_in_q=wi[:, ac:ac + bw].astype(BF16),
        w_in_k=wi[:, ac + bw:ac + 2 * bw].astype(BF16),
        w_in_v=wi[:, ac + 2 * bw:ac + 3 * bw].astype(BF16),
        w_in_g=wi[:, ac + 3 * bw:].astype(BF16),
        mu=mu_shift[l].reshape(1, ac), w0=w0[l].reshape(1, aw), a0=a0[l].reshape(1, aw),
        k_k=k_k[l].reshape(1, aw), k_a=k_a[l].reshape(1, aw), r_k=r_k[l].reshape(1, aw),
        wwa=wwa, g2=g2[l].astype(BF16),
        lnx_w=lnx_w[l].reshape(1, aw), lnx_b=lnx_b[l].reshape(1, aw),
        w_out_a=w_out[l][:aw].astype(BF16), w_out_b=w_out[l][aw:].astype(BF16),
        w_ffn_in=w_ffn_in[l].astype(BF16), w_ffn_out=w_ffn_out[l].astype(BF16),
    )

    def moba_prompt_fn(q, k, v):
        kb, vt, km = _moba_prep(k[0], v[0])
        return _moba_prompt(q[0], kb, vt, km[:, 0, :])[None]

    def moba_sample_fn(q, k, v):
        o = _moba_sample(q, k, v, cache_k, cache_v, page_table)
        return o.reshape(Bs, nh_b, Ts, B_HEAD_DIM).transpose(0, 2, 1, 3).reshape(Bs, Ts, bw)

    zero_shift = jnp.zeros((Bp, ac), F32)
    zero_wkv = jnp.zeros((Bp, nh_a, A_HEAD_DIM, A_HEAD_DIM), F32)
    yp, shp, wkvp, kp, vp = _trunk(x_prompt, mod_p, zero_shift, zero_wkv, moba_prompt_fn, W, 1, 512)
    ys, shs, wkvs, ksm, vsm = _trunk(x_sample, mod_s, state_shift[l], state_wkv[l], moba_sample_fn, W, Bs, Ts)

    hd = lambda z, B, T: z.reshape(1, B, T, nh_b, B_HEAD_DIM)
    return (yp, ys, hd(kp, Bp, Tp), hd(vp, Bp, Tp), wkvp[None], shp[None],
            hd(ksm, Bs, Ts), hd(vsm, Bs, Ts), wkvs[None], shs[None])
```

```python
import functools

import numpy as np
import jax
import jax.numpy as jnp
from jax import lax
from jax.experimental import pallas as pl
from jax.experimental.pallas import tpu as pltpu

F32 = jnp.float32
BF16 = jnp.bfloat16

A_HEAD_DIM = 64
DECAY_LORA = 64
AAA_LORA = 64
GATE_LORA = 128
B_HEAD_DIM = 128
MOBA_BLOCK = 256
MOBA_TOPK = 3
PAGE_SIZE = 128
LNX_EPS = 64e-5
RMS_EPS = 1e-6
NEG = -1e30

LANES = 128
SUBLANES = 8
VMEM_LIMIT = 56 * 1024 * 1024
VMEM_LIMIT_ATTN = 60 * 1024 * 1024


def _cparams(sem):
    return pltpu.CompilerParams(dimension_semantics=sem, vmem_limit_bytes=VMEM_LIMIT)


def _split_bf16(x):
    hi = x.astype(BF16)
    lo = (x - hi.astype(F32)).astype(BF16)
    return hi, lo


def _seg_sum128(x, m2):
    hi, lo = _split_bf16(x)
    return jnp.dot(jnp.concatenate([hi, lo], axis=1), m2, preferred_element_type=F32)


def _seg_sum(x, m2):
    n = x.shape[1] // LANES
    return jnp.concatenate([_seg_sum128(x[:, c * LANES:(c + 1) * LANES], m2) for c in range(n)], axis=1)


def _block_ones_np():
    i = np.arange(LANES)
    return (i[:, None] // A_HEAD_DIM == i[None, :] // A_HEAD_DIM).astype(np.float32)


def _block_ones():
    m = _block_ones_np()
    return jnp.asarray(np.concatenate([m, m], axis=0), dtype=BF16)


def _ada_kernel(c_ref, w_ref, b_ref, o_ref):
    c = c_ref[...]
    s = (c * jax.nn.sigmoid(c)).astype(BF16)
    o_ref[...] = jnp.dot(s, w_ref[...].astype(BF16), preferred_element_type=F32) + b_ref[...]


def _ada(c, w_ada, b_ada):
    m, d = c.shape
    n = w_ada.shape[1]
    tn = 1024
    return pl.pallas_call(
        _ada_kernel,
        grid=(n // tn,),
        in_specs=[pl.BlockSpec((m, d), lambda j: (0, 0)),
                  pl.BlockSpec((d, tn), lambda j: (0, j)),
                  pl.BlockSpec((1, tn), lambda j: (0, j))],
        out_specs=pl.BlockSpec((m, tn), lambda j: (0, j)),
        out_shape=jax.ShapeDtypeStruct((m, n), F32),
        compiler_params=_cparams(("arbitrary",)),
        name="ada_mod",
    )(c, w_ada, b_ada.reshape(1, n))


def _norm_mod(x, g, sc, sh):
    ms = jnp.mean(x * x, axis=-1, keepdims=True)
    y = x * lax.rsqrt(ms + RMS_EPS) * g
    return y * (1.0 + sc) + sh


def _nmm_kernel(x_ref, g_ref, sc_ref, sh_ref, w_ref, o_ref, h_ref):
    gb, tm, d = x_ref.shape

    @pl.when(pl.program_id(2) == 0)
    def _():
        h = _norm_mod(x_ref[...], g_ref[...], sc_ref[...], sh_ref[...])
        h_ref[...] = h.reshape(gb * tm, d).astype(BF16)

    o = jnp.dot(h_ref[...], w_ref[...], preferred_element_type=F32)
    o_ref[...] = o.reshape(o_ref.shape).astype(o_ref.dtype)


def _norm_mod_matmul(x, g, sc, sh, w, gb, tm, tn, out_dtype=F32):
    G, T, D = x.shape
    N = w.shape[1]
    return pl.pallas_call(
        _nmm_kernel,
        grid=(G // gb, T // tm, N // tn),
        in_specs=[pl.BlockSpec((gb, tm, D), lambda a, i, j: (a, i, 0)),
                  pl.BlockSpec((1, 1, D), lambda a, i, j: (0, 0, 0)),
                  pl.BlockSpec((gb, 1, D), lambda a, i, j: (a, 0, 0)),
                  pl.BlockSpec((gb, 1, D), lambda a, i, j: (a, 0, 0)),
                  pl.BlockSpec((D, tn), lambda a, i, j: (0, j))],
        out_specs=pl.BlockSpec((gb, tm, tn), lambda a, i, j: (a, i, j)),
        out_shape=jax.ShapeDtypeStruct((G, T, N), out_dtype),
        scratch_shapes=[pltpu.VMEM((gb * tm, D), BF16)],
        compiler_params=_cparams(("arbitrary", "arbitrary", "arbitrary")),
        name="norm_mod_matmul",
    )(x, g, sc, sh, w)


def _softplus(z):
    return jnp.maximum(z, 0.0) + jnp.log(1.0 + jnp.exp(-jnp.abs(z)))


def _prep_kernel(pa_ref, sh0_ref, mu_ref, w0_ref, a0_ref, kk_ref, ka_ref, rk_ref, wwa_ref, g2_ref, m2_ref,
                 r_o, w_o, k_o, v_o, a_o, b_o, c_o, g_o, bon_o, last_o, carry_ref):
    tm = pa_ref.shape[1]
    aw = r_o.shape[2]

    @pl.when(pl.program_id(1) == 0)
    def _():
        carry_ref[...] = sh0_ref[0]

    pa = pa_ref[0]
    rolled = pltpu.roll(pa, 1, axis=0)
    row = lax.broadcasted_iota(jnp.int32, (tm, 1), 0)
    prev = jnp.where(row == 0, carry_ref[...], rolled)
    last = pa[tm - 1:tm, :]
    carry_ref[...] = last
    last_o[0] = last

    ps = pa + (prev - pa) * mu_ref[...]
    r = ps[:, 0:aw]
    k = ps[:, aw:2 * aw]
    v = ps[:, 2 * aw:3 * aw]
    o3 = 3 * aw
    x_wa = ps[:, o3:o3 + DECAY_LORA + AAA_LORA]
    gl = ps[:, o3 + DECAY_LORA + AAA_LORA:]
    lane = lax.broadcasted_iota(jnp.int32, x_wa.shape, 1)
    z = jnp.where(lane < DECAY_LORA, jnp.tanh(x_wa), x_wa).astype(BF16)
    wa = jnp.dot(z, wwa_ref[...], preferred_element_type=F32)
    lw = wa[:, :aw]
    la = wa[:, aw:]
    g = jnp.dot(jax.nn.sigmoid(gl).astype(BF16), g2_ref[...], preferred_element_type=F32)

    w_log = -_softplus(-(w0_ref[...] + lw)) - 0.5
    decay = jnp.exp(-jnp.exp(w_log))
    a = jax.nn.sigmoid(a0_ref[...] + la)
    m2 = m2_ref[...]
    kk = k * kk_ref[...]
    kk = kk * lax.rsqrt(jnp.maximum(_seg_sum(kk * kk, m2), 1e-24))
    k_mod = k * (1.0 + (a - 1.0) * ka_ref[...])
    bonus = _seg_sum(r * k_mod * rk_ref[...], m2) * v

    a_vec = -kk
    b_vec = kk * a
    a_next = pltpu.roll(a_vec, tm - 1, axis=0)
    r_o[0] = r
    w_o[0] = decay
    k_o[0] = k_mod
    v_o[0] = v
    a_o[0] = a_vec
    b_o[0] = b_vec
    c_o[0] = _seg_sum(b_vec * a_next, m2)
    g_o[0] = g
    bon_o[0] = bonus


def _rwkv_prep(pa, shift0, mu, w0, a0, k_k, k_a, r_k, wwa, g2, m2, tm):
    G, T, AC = pa.shape
    aw = w0.shape[-1]
    vec = lambda n: pl.BlockSpec((1, n), lambda a, i: (0, 0))
    full = lambda s: pl.BlockSpec(s, lambda a, i: (0, 0))
    big = pl.BlockSpec((1, tm, aw), lambda a, i: (a, i, 0))
    assert tm % 2 == 0
    outs = [jax.ShapeDtypeStruct((G, T, aw), F32)] * 9 + [jax.ShapeDtypeStruct((G, 1, AC), F32)]
    return pl.pallas_call(
        _prep_kernel,
        grid=(G, T // tm),
        in_specs=[pl.BlockSpec((1, tm, AC), lambda a, i: (a, i, 0)),
                  pl.BlockSpec((1, 1, AC), lambda a, i: (a, 0, 0)),
                  vec(AC), vec(aw), vec(aw), vec(aw), vec(aw), vec(aw),
                  full(wwa.shape), full(g2.shape), full(m2.shape)],
        out_specs=[big] * 9 + [pl.BlockSpec((1, 1, AC), lambda a, i: (a, 0, 0))],
        out_shape=outs,
        scratch_shapes=[pltpu.VMEM((1, AC), F32)],
        compiler_params=_cparams(("arbitrary", "arbitrary")),
        name="rwkv_prep",
    )(pa, shift0, mu, w0, a0, k_k, k_a, r_k, wwa, g2, m2)


def _half_lane_unzip(xt):
    lo_half = lax.broadcasted_iota(jnp.int32, (A_HEAD_DIM, LANES), 1) < A_HEAD_DIM
    top = xt[0:A_HEAD_DIM]
    bot = xt[A_HEAD_DIM:2 * A_HEAD_DIM]
    return (jnp.where(lo_half, top, pltpu.roll(bot, A_HEAD_DIM, axis=1)),
            jnp.where(lo_half, pltpu.roll(top, A_HEAD_DIM, axis=1), bot))


def _scan_kernel(r_ref, w_ref, k_ref, a_ref, b_ref, c_ref, v2_ref, s0_ref, my_ref, e_ref, y_ref, sf_ref, s_ref,
                 vt_ref, q_ref, vb_ref):
    C = r_ref.shape[1]
    npair = r_ref.shape[2]
    rows = npair * A_HEAD_DIM

    @pl.when(pl.program_id(1) == 0)
    def _():
        s_ref[...] = s0_ref[0]

    v2 = v2_ref[0]
    if C < A_HEAD_DIM:
        v2 = jnp.concatenate([v2, jnp.zeros((A_HEAD_DIM - C, v2.shape[1]), F32)], axis=0)
    for pp in range(npair // 2):
        x = jnp.concatenate([v2[:, (2 * pp) * LANES:(2 * pp + 1) * LANES],
                             v2[:, (2 * pp + 1) * LANES:(2 * pp + 2) * LANES]], axis=0)
        for j, o in enumerate(_half_lane_unzip(x.T)):
            hi, lo = _split_bf16(o)
            p = 2 * pp + j
            vt_ref[p * A_HEAD_DIM:(p + 1) * A_HEAD_DIM, :] = jnp.concatenate([hi, lo], axis=1)

    y_ref[...] = jnp.zeros(y_ref.shape, F32)
    q_ref[...] = jnp.zeros(q_ref.shape, BF16)
    my = my_ref[...]
    tlane = lax.broadcasted_iota(jnp.int32, (rows, LANES), 1) % A_HEAD_DIM

    def bc(ref, t):
        return jnp.concatenate(
            [jnp.broadcast_to(ref[0, t, p:p + 1, :], (A_HEAD_DIM, LANES)) for p in range(npair)], axis=0)

    def put_y(j):
        yb = jnp.dot(q_ref[...], my, preferred_element_type=F32)
        y_ref[0, 0] = jnp.where(tlane == 2 * j, yb[:, :LANES],
                                jnp.where(tlane == 2 * j + 1, yb[:, LANES:], y_ref[0, 0]))

    def two_steps(j, carry):
        t0 = 2 * j
        t1 = t0 + 1
        vb = vb_ref[...]
        S0 = s_ref[...]
        X = S0 * bc(w_ref, t0) + vb[:, :LANES] * bc(k_ref, t0)
        lhs = jnp.concatenate([(S0 * bc(a_ref, t0)).astype(BF16), (X * bc(a_ref, t1)).astype(BF16)], axis=1)
        out = jnp.dot(lhs, my, preferred_element_type=F32)
        put_y(j - 1)
        vb_next = jnp.dot(vt_ref[...], e_ref[jnp.minimum(j + 1, C // 2 - 1)], preferred_element_type=F32)
        sa0 = out[:, :LANES]
        sa1 = out[:, LANES:] + sa0 * bc(c_ref, t0)
        S1 = X + sa0 * bc(b_ref, t0)
        S2 = S1 * bc(w_ref, t1) + sa1 * bc(b_ref, t1) + vb[:, LANES:] * bc(k_ref, t1)
        s_ref[...] = S2
        q_ref[...] = jnp.concatenate([(S1 * bc(r_ref, t0)).astype(BF16), (S2 * bc(r_ref, t1)).astype(BF16)], axis=1)
        vb_ref[...] = vb_next
        return carry

    vb_ref[...] = jnp.dot(vt_ref[...], e_ref[0], preferred_element_type=F32)
    lax.fori_loop(0, C // 2, two_steps, 0, unroll=2)
    put_y(C // 2 - 1)
    sf_ref[0] = s_ref[...]


def _onehot_columns():
    j = np.arange(A_HEAD_DIM // 2)[:, None, None, None]
    i = np.arange(2 * LANES)[None, :, None, None] % LANES
    c = np.arange(2)[None, None, :, None]
    l = np.arange(LANES)[None, None, None, :]
    e = i == (l // A_HEAD_DIM) * A_HEAD_DIM + 2 * j + c
    return jnp.asarray(e.reshape(A_HEAD_DIM // 2, 2 * LANES, 2 * LANES), dtype=BF16)


def _block_ones_pair():
    m = _block_ones_np()
    z = np.zeros_like(m)
    return jnp.asarray(np.block([[m, z], [z, m]]), dtype=BF16)


def _wkv_scan(r, w, k, a, b, c, v2, s0, C):
    G, T, npair, _ = r.shape
    rows = npair * A_HEAD_DIM
    assert C % 2 == 0
    e = _onehot_columns()[:C // 2]
    my = _block_ones_pair()
    seq = pl.BlockSpec((1, C, npair, LANES), lambda g, c: (g, c, 0, 0))
    st = pl.BlockSpec((1, rows, LANES), lambda g, c: (g, 0, 0))
    return pl.pallas_call(
        _scan_kernel,
        grid=(G, T // C),
        in_specs=[seq] * 6 + [pl.BlockSpec((1, C, v2.shape[2]), lambda g, c: (g, c, 0)), st,
                              pl.BlockSpec(my.shape, lambda g, c: (0, 0)),
                              pl.BlockSpec(e.shape, lambda g, c: (0, 0, 0))],
        out_specs=[pl.BlockSpec((1, 1, rows, LANES), lambda g, c: (g, c, 0, 0)), st],
        out_shape=[jax.ShapeDtypeStruct((G, T // C, rows, LANES), F32),
                   jax.ShapeDtypeStruct((G, rows, LANES), F32)],
        scratch_shapes=[pltpu.VMEM((rows, LANES), F32), pltpu.VMEM((rows, 2 * LANES), BF16),
                        pltpu.VMEM((rows, 2 * LANES), BF16), pltpu.VMEM((rows, 2 * LANES), F32)],
        compiler_params=_cparams(("arbitrary", "arbitrary")),
        name="wkv_scan",
    )(r, w, k, a, b, c, v2, s0, my, e)


def _post_kernel(yt_ref, g_ref, bon_ref, lw_ref, lb_ref, m2_ref, o_ref):
    C = o_ref.shape[1]
    rows = yt_ref.shape[2]
    chunks = []
    for pp in range(rows // LANES):
        rt = yt_ref[0, 0, pp * LANES:(pp + 1) * LANES, :].T
        chunks.extend(o[:C] for o in _half_lane_unzip(rt))
    y = jnp.concatenate(chunks, axis=1)
    m2 = m2_ref[...]
    inv = 1.0 / A_HEAD_DIM
    mean = _seg_sum(y, m2) * inv
    yc = y - mean
    var = _seg_sum(yc * yc, m2) * inv
    yn = yc * lax.rsqrt(var + LNX_EPS) * lw_ref[...] + lb_ref[...]
    o_ref[0] = ((yn + bon_ref[0]) * g_ref[0]).astype(o_ref.dtype)


def _rwkv_post(yt, g, bon, lnx_w, lnx_b, m2):
    G, NC, rows, _ = yt.shape
    T, aw = g.shape[1], g.shape[2]
    C = T // NC
    big = pl.BlockSpec((1, C, aw), lambda a, c: (a, c, 0))
    vec = pl.BlockSpec((1, aw), lambda a, c: (0, 0))
    return pl.pallas_call(
        _post_kernel,
        grid=(G, NC),
        in_specs=[pl.BlockSpec((1, 1, rows, LANES), lambda a, c: (a, c, 0, 0)), big, big, vec, vec,
                  pl.BlockSpec(m2.shape, lambda a, c: (0, 0))],
        out_specs=big,
        out_shape=jax.ShapeDtypeStruct((G, T, aw), F32),
        compiler_params=_cparams(("arbitrary", "arbitrary")),
        name="rwkv_post",
    )(yt, g, bon, lnx_w, lnx_b, m2)


def _moba_prep_kernel(k_ref, v_ref, kb_ref, vt_ref, km_ref):
    k = k_ref[...]
    kb_ref[...] = k.astype(BF16)
    km_ref[0] = jnp.mean(k, axis=0, keepdims=True)
    vt_ref[...] = v_ref[...].T.astype(BF16)


def _moba_prep(k, v):
    T, W = k.shape
    nb = T // MOBA_BLOCK
    blk = pl.BlockSpec((MOBA_BLOCK, W), lambda n: (n, 0))
    return pl.pallas_call(
        _moba_prep_kernel,
        grid=(nb,),
        in_specs=[blk, blk],
        out_specs=[blk, pl.BlockSpec((W, MOBA_BLOCK), lambda n: (0, n)),
                   pl.BlockSpec((1, 1, W), lambda n: (n, 0, 0))],
        out_shape=[jax.ShapeDtypeStruct((T, W), BF16), jax.ShapeDtypeStruct((W, T), BF16),
                   jax.ShapeDtypeStruct((nb, 1, W), F32)],
        compiler_params=_cparams(("arbitrary",)),
        name="moba_prep",
    )(k, v)


def _select_topk(gate, valid, axis):
    idx = lax.broadcasted_iota(jnp.int32, gate.shape, axis)
    g = jnp.where(valid, gate, NEG)
    sel = jnp.zeros(gate.shape, jnp.bool_)
    for _ in range(MOBA_TOPK):
        m = jnp.max(g, axis=axis, keepdims=True)
        first = jnp.min(jnp.where(g == m, idx, jnp.int32(2 ** 30)), axis=axis, keepdims=True)
        pick = idx == first
        sel = jnp.logical_or(sel, pick)
        g = jnp.where(pick, -jnp.inf, g)
    return jnp.logical_and(sel, valid)


PROMPT_HEADS_PER_STEP = 2
PROMPT_UNROLL = 4
LOG2E = 1.4426950408889634


def _moba_prompt_kernel(q_ref, k_ref, vt_ref, km_ref, o_ref, sel_ref, s_ref):
    i = pl.program_id(1)
    tq = q_ref.shape[0]
    nb = km_ref.shape[0]
    dh = B_HEAD_DIM
    nt = (((1,), (1,)), ((), ()))
    heads = range(PROMPT_HEADS_PER_STEP)
    blk = lax.broadcasted_iota(jnp.int32, (nb, tq), 0)
    key = lax.broadcasted_iota(jnp.int32, (MOBA_BLOCK, tq), 0)
    qi = lax.broadcasted_iota(jnp.int32, (MOBA_BLOCK, tq), 1)
    start = pl.multiple_of(i * MOBA_BLOCK, MOBA_BLOCK)

    qbs = []
    for u in heads:
        sl = slice(u * dh, (u + 1) * dh)
        q = q_ref[:, sl]
        qb = (q * (dh ** -0.5 * LOG2E)).astype(BF16)
        qbs.append(qb)
        gate_t = lax.dot_general(km_ref[:, sl], q, nt, precision=lax.Precision.HIGHEST,
                                 preferred_element_type=F32)
        sel_ref[u, 0:nb, :] = _select_topk(gate_t, blk < i, 0).astype(F32)
        sel_ref[u, nb:, :] = jnp.zeros((sel_ref.shape[1] - nb, tq), F32)

    def scores(n, u):
        st = pl.multiple_of(n * MOBA_BLOCK, MOBA_BLOCK)
        return lax.dot_general(k_ref[pl.ds(st, MOBA_BLOCK), u * dh:(u + 1) * dh], qbs[u], nt,
                               preferred_element_type=F32)

    def selected(n, u):
        return sel_ref[u, pl.ds(n, 1), :] > 0.5

    gk = PROMPT_UNROLL * MOBA_BLOCK
    n_iter = (i + PROMPT_UNROLL - 1) // PROMPT_UNROLL

    def pass1(g, ms):
        st = pl.multiple_of(g * gk, gk)
        out = []
        for u in heads:
            m = ms[u]
            s_g = lax.dot_general(k_ref[pl.ds(st, gk), u * dh:(u + 1) * dh], qbs[u], nt,
                                  preferred_element_type=F32)
            s_ref[u, pl.ds(st, gk), :] = s_g
            for c in range(PROMPT_UNROLL):
                mb = jnp.max(s_g[c * MOBA_BLOCK:(c + 1) * MOBA_BLOCK], axis=0, keepdims=True)
                m = jnp.maximum(m, jnp.where(selected(g * PROMPT_UNROLL + c, u), mb, NEG))
            out.append(m)
        return tuple(out)

    ms = lax.fori_loop(0, n_iter, pass1, tuple(jnp.full((1, tq), NEG, F32) for _ in heads))

    init = []
    for u in heads:
        sl = slice(u * dh, (u + 1) * dh)
        s_t = jnp.where(key <= qi, scores(i, u), NEG)
        m = jnp.maximum(ms[u], jnp.max(s_t, axis=0, keepdims=True))
        p = jnp.exp2(s_t - m)
        l = jnp.sum(p, axis=0, keepdims=True)
        acc = jnp.dot(vt_ref[sl, pl.ds(start, MOBA_BLOCK)], p.astype(BF16), preferred_element_type=F32)
        init.append((m, l, acc))
    ms = tuple(m for m, _, _ in init)

    def pass2(g, carry):
        st = pl.multiple_of(g * gk, gk)
        out = []
        for u in heads:
            sl = slice(u * dh, (u + 1) * dh)
            l, acc = carry[u]
            ps = []
            for c in range(PROMPT_UNROLL):
                n = g * PROMPT_UNROLL + c
                s_t = s_ref[u, pl.ds(pl.multiple_of(n * MOBA_BLOCK, MOBA_BLOCK), MOBA_BLOCK), :]
                p = jnp.exp2(s_t - jnp.where(selected(n, u), ms[u], -NEG))
                l = l + jnp.sum(p, axis=0, keepdims=True)
                ps.append(p.astype(BF16))
            acc = acc + jnp.dot(vt_ref[sl, pl.ds(st, gk)], jnp.concatenate(ps, axis=0),
                                preferred_element_type=F32)
            out.append((l, acc))
        return tuple(out)

    fin = lax.fori_loop(0, n_iter, pass2, tuple((l, acc) for _, l, acc in init))
    for u in heads:
        l, acc = fin[u]
        o_ref[:, u * dh:(u + 1) * dh] = (acc / l).T


def _moba_prompt(q, kb, vt, kmean):
    T, W = q.shape
    hw = PROMPT_HEADS_PER_STEP * B_HEAD_DIM
    nb = T // MOBA_BLOCK
    tp = T + (PROMPT_UNROLL - 1) * MOBA_BLOCK
    kb = jnp.pad(kb, ((0, tp - T), (0, 0)))
    vt = jnp.pad(vt, ((0, 0), (0, tp - T)))
    once = pl.Buffered(1)
    return pl.pallas_call(
        _moba_prompt_kernel,
        grid=(W // hw, nb),
        in_specs=[pl.BlockSpec((MOBA_BLOCK, hw), lambda h, i: (i, h)),
                  pl.BlockSpec((tp, hw), lambda h, i: (0, h), pipeline_mode=once),
                  pl.BlockSpec((hw, tp), lambda h, i: (h, 0), pipeline_mode=once),
                  pl.BlockSpec((nb, hw), lambda h, i: (0, h))],
        out_specs=pl.BlockSpec((MOBA_BLOCK, hw), lambda h, i: (i, h)),
        out_shape=jax.ShapeDtypeStruct((T, W), F32),
        scratch_shapes=[pltpu.VMEM((PROMPT_HEADS_PER_STEP, nb + SUBLANES, MOBA_BLOCK), F32),
                        pltpu.VMEM((PROMPT_HEADS_PER_STEP, tp, MOBA_BLOCK), F32)],
        compiler_params=pltpu.CompilerParams(dimension_semantics=("arbitrary", "arbitrary"),
                                             vmem_limit_bytes=VMEM_LIMIT_ATTN),
        name="moba_prompt",
    )(q, kb, vt, kmean)


SAMPLE_PAGES_PER_STEP = 8


def _moba_sample_kernel(pt_ref, q_ref, kn_ref, vn_ref, *rest):
    P = SAMPLE_PAGES_PER_STEP
    kp = rest[:P]
    vp = rest[P:2 * P]
    o_ref = rest[2 * P]
    km_ref, m_ref, l_ref, oacc_ref = rest[2 * P + 1:]
    s = pl.program_id(1)
    ns = pl.num_programs(1)
    tq, W = q_ref.shape[1], q_ref.shape[2]
    dh = B_HEAD_DIM
    nh = W // dh
    hq = nh * tq
    nbp = km_ref.shape[1]
    ppb = MOBA_BLOCK // PAGE_SIZE
    bps = P // ppb
    nt = (((1,), (1,)), ((), ()))

    def by_head(x):
        return jnp.concatenate([x[:, h * dh:(h + 1) * dh] for h in range(nh)], axis=0)

    qall = by_head(q_ref[0])
    qall_b = (qall * (dh ** -0.5)).astype(BF16)

    def partial(k_b, v_b, mask):
        sc = jnp.where(mask, lax.dot_general(qall_b, k_b, nt, preferred_element_type=F32), NEG)
        m = jnp.max(sc, axis=1, keepdims=True)
        p = jnp.exp(sc - m)
        l = jnp.sum(p, axis=1, keepdims=True)
        o = jnp.dot(p.astype(BF16), v_b, preferred_element_type=F32)
        return jnp.broadcast_to(m, (hq, dh)), jnp.broadcast_to(l, (hq, dh)), o

    nflat = MOBA_BLOCK * nh
    rowh = lax.broadcasted_iota(jnp.int32, (hq, nflat), 0) // tq
    colh = lax.broadcasted_iota(jnp.int32, (hq, nflat), 1) % nh
    same_head = rowh == colh
    for j in range(bps):
        kpages = [kp[j * ppb + u][0, 0] for u in range(ppb)]
        vpages = [vp[j * ppb + u][0, 0] for u in range(ppb)]
        n = s * bps + j
        kmean = sum(jnp.sum(pg, axis=0) for pg in kpages) * (1.0 / MOBA_BLOCK)
        for h in range(nh):
            km_ref[h, pl.ds(n, 1), :] = kmean[h:h + 1, :]
        kflat = jnp.concatenate([pg.reshape(PAGE_SIZE * nh, dh) for pg in kpages], axis=0).astype(BF16)
        vflat = jnp.concatenate([pg.reshape(PAGE_SIZE * nh, dh) for pg in vpages], axis=0).astype(BF16)
        m, l, o = partial(kflat, vflat, same_head)
        m_ref[n] = m
        l_ref[n] = l
        oacc_ref[n] = o

    @pl.when(s == ns - 1)
    def _():
        pad = jnp.zeros((LANES - hq, dh), F32)
        kn = jnp.concatenate([by_head(kn_ref[0]), pad], axis=0).astype(BF16)
        vn = jnp.concatenate([by_head(vn_ref[0]), pad], axis=0).astype(BF16)
        col = lax.broadcasted_iota(jnp.int32, (hq, LANES), 1)
        row = lax.broadcasted_iota(jnp.int32, (hq, LANES), 0)
        own_mask = jnp.logical_and(col // tq == row // tq, col % tq <= row % tq)
        m_own, l_own, o_own = partial(kn, vn, own_mask)

        gate = jnp.concatenate(
            [lax.dot_general(qall[h * tq:(h + 1) * tq], km_ref[h], nt, precision=lax.Precision.HIGHEST,
                             preferred_element_type=F32) for h in range(nh)], axis=0)
        sel = _select_topk(gate, jnp.ones(gate.shape, jnp.bool_), 1).astype(F32)
        mx = m_own
        for n in range(nbp):
            sn = sel[:, n:n + 1] > 0.5
            mx = jnp.where(sn, jnp.maximum(mx, m_ref[n]), mx)
        w_own = jnp.exp(m_own - mx)
        lsum = w_own * l_own
        osum = w_own * o_own
        for n in range(nbp):
            sn = sel[:, n:n + 1] > 0.5
            wn = jnp.where(sn, jnp.exp(m_ref[n] - mx), 0.0)
            lsum = lsum + wn * l_ref[n]
            osum = osum + wn * oacc_ref[n]
        o_ref[0] = osum / lsum


def _moba_sample(q, kn, vn, cache_k, cache_v, page_table):
    B, tq, W = q.shape
    nh = W // B_HEAD_DIM
    hq = nh * tq
    assert hq <= LANES
    n_pages = page_table.shape[1]
    P = SAMPLE_PAGES_PER_STEP
    nbp = n_pages * PAGE_SIZE // MOBA_BLOCK
    small = pl.BlockSpec((1, tq, W), lambda b, s, pt: (b, 0, 0))

    def page_spec(j):
        return pl.BlockSpec((1, 1, PAGE_SIZE, nh, B_HEAD_DIM), lambda b, s, pt: (0, pt[b, s * P + j], 0, 0, 0))

    grid_spec = pltpu.PrefetchScalarGridSpec(
        num_scalar_prefetch=1,
        grid=(B, n_pages // P),
        in_specs=[small, small, small] + [page_spec(j) for j in range(P)] * 2,
        out_specs=pl.BlockSpec((1, hq, B_HEAD_DIM), lambda b, s, pt: (b, 0, 0)),
        scratch_shapes=[pltpu.VMEM((nh, nbp, B_HEAD_DIM), F32),
                        pltpu.VMEM((nbp, hq, B_HEAD_DIM), F32), pltpu.VMEM((nbp, hq, B_HEAD_DIM), F32),
                        pltpu.VMEM((nbp, hq, B_HEAD_DIM), F32)],
    )
    return pl.pallas_call(
        _moba_sample_kernel,
        grid_spec=grid_spec,
        out_shape=jax.ShapeDtypeStruct((B, hq, B_HEAD_DIM), F32),
        compiler_params=_cparams(("arbitrary", "arbitrary")),
        name="moba_sample",
    )(page_table, q, kn, vn, *([cache_k] * P), *([cache_v] * P))


def _mixout_kernel(x_ref, oa_ref, ob_ref, pg_ref, gt_ref, gp_ref, wa_ref, wb_ref, o_ref):
    gb, tm, d = x_ref.shape
    aw = oa_ref.shape[2]
    pg = pg_ref[...]
    ma = (oa_ref[...] * jax.nn.sigmoid(pg[..., :aw])).reshape(gb * tm, aw).astype(BF16)
    mb = (ob_ref[...] * jax.nn.sigmoid(pg[..., aw:])).reshape(gb * tm, pg.shape[2] - aw).astype(BF16)
    f = jnp.dot(ma, wa_ref[...], preferred_element_type=F32) + jnp.dot(mb, wb_ref[...], preferred_element_type=F32)
    y = f * lax.rsqrt(jnp.mean(f * f, axis=-1, keepdims=True) + RMS_EPS)
    y = y.reshape(gb, tm, d) * gp_ref[...]
    o_ref[...] = x_ref[...] + gt_ref[...] * y


def _mixout(x, oa, ob, pg, gt, gpost, wa, wb, gb, tm):
    G, T, D = x.shape
    aw, bw = oa.shape[2], ob.shape[2]
    row = lambda n: pl.BlockSpec((gb, tm, n), lambda a, i: (a, i, 0))
    return pl.pallas_call(
        _mixout_kernel,
        grid=(G // gb, T // tm),
        in_specs=[row(D), row(aw), row(bw), row(aw + bw),
                  pl.BlockSpec((gb, 1, D), lambda a, i: (a, 0, 0)),
                  pl.BlockSpec((1, 1, D), lambda a, i: (0, 0, 0)),
                  pl.BlockSpec(wa.shape, lambda a, i: (0, 0)),
                  pl.BlockSpec(wb.shape, lambda a, i: (0, 0))],
        out_specs=row(D),
        out_shape=jax.ShapeDtypeStruct((G, T, D), F32),
        compiler_params=_cparams(("arbitrary", "arbitrary")),
        name="mix_out",
    )(x, oa, ob, pg, gt, gpost, wa, wb)


def _ffn_kernel(x_ref, gpre_ref, sc_ref, sh_ref, gt_ref, gpost_ref, wg_ref, wu_ref, wo_ref, o_ref, h_ref, acc_ref):
    gb, tm, d = x_ref.shape
    j = pl.program_id(2)

    @pl.when(j == 0)
    def _():
        h = _norm_mod(x_ref[...], gpre_ref[...], sc_ref[...], sh_ref[...])
        h_ref[...] = h.reshape(gb * tm, d).astype(BF16)
        acc_ref[...] = jnp.zeros(acc_ref.shape, F32)

    h = h_ref[...]
    g = jnp.dot(h, wg_ref[...], preferred_element_type=F32)
    u = jnp.dot(h, wu_ref[...], preferred_element_type=F32)
    act = (g * jax.nn.sigmoid(g) * u).astype(BF16)
    acc_ref[...] += jnp.dot(act, wo_ref[...], preferred_element_type=F32)

    @pl.when(j == pl.num_programs(2) - 1)
    def _():
        f = acc_ref[...]
        y = f * lax.rsqrt(jnp.mean(f * f, axis=-1, keepdims=True) + RMS_EPS)
        y = y.reshape(gb, tm, d) * gpost_ref[...]
        o_ref[...] = x_ref[...] + gt_ref[...] * y


def _ffn(x, gpre, sc, sh, gt, gpost, w_in, w_out, gb, tm, tf):
    G, T, D = x.shape
    dff = w_out.shape[0]
    nj = dff // tf
    row = pl.BlockSpec((gb, tm, D), lambda a, i, j: (a, i, 0))
    mod = pl.BlockSpec((gb, 1, D), lambda a, i, j: (a, 0, 0))
    gain = pl.BlockSpec((1, 1, D), lambda a, i, j: (0, 0, 0))
    return pl.pallas_call(
        _ffn_kernel,
        grid=(G // gb, T // tm, nj),
        in_specs=[row, gain, mod, mod, mod, gain,
                  pl.BlockSpec((D, tf), lambda a, i, j: (0, j)),
                  pl.BlockSpec((D, tf), lambda a, i, j: (0, j + nj)),
                  pl.BlockSpec((tf, D), lambda a, i, j: (j, 0))],
        out_specs=row,
        out_shape=jax.ShapeDtypeStruct((G, T, D), F32),
        scratch_shapes=[pltpu.VMEM((gb * tm, D), BF16), pltpu.VMEM((gb * tm, D), F32)],
        compiler_params=_cparams(("arbitrary", "arbitrary", "arbitrary")),
        name="ffn",
    )(x, gpre, sc, sh, gt, gpost, w_in, w_in, w_out)


def _pair_state(s):
    B, H, n, _ = s.shape
    return s.reshape(B, H // 2, 2, n, n).transpose(0, 1, 3, 2, 4).reshape(B, (H // 2) * n, 2 * n)


def _unpair_state(s, H):
    B = s.shape[0]
    n = A_HEAD_DIM
    return s.reshape(B, H // 2, n, 2, n).transpose(0, 1, 3, 2, 4).reshape(B, H, n, n)


def _trunk(x, mod, shift0, wkv0, moba_fn, W, gb, tm):
    G, T, D = x.shape
    aw = W["w0"].shape[-1]
    bw = D - aw
    ac = W["mu"].shape[-1]
    m2 = W["m2"]
    md = [mod[:, c:c + 1, :] for c in range(6)]
    sh_m, sc_m, gt_m, sh_f, sc_f, gt_f = md

    proj = lambda w, tn: _norm_mod_matmul(x, W["g_pre_mix"], sc_m, sh_m, w, gb, tm, tn)
    pa = proj(W["w_in_a"], ac // 2)
    q = proj(W["w_in_q"], bw)
    k = proj(W["w_in_k"], bw)
    v = proj(W["w_in_v"], bw)
    pg = proj(W["w_in_g"], 1024)

    tp = min(T, 256)
    r_, w_, k_, v_, a_, b_, c_, g_, bon, last = _rwkv_prep(
        pa, shift0[:, None, :], W["mu"], W["w0"], W["a0"], W["k_k"], W["k_a"], W["r_k"], W["wwa"], W["g2"], m2, tp)
    C = min(T, A_HEAD_DIM)
    npair = aw // LANES
    rs = lambda z: z.reshape(G, T, npair, LANES)
    yt, s_fin = _wkv_scan(rs(r_), rs(w_), rs(k_), rs(a_), rs(b_), rs(c_), v_, _pair_state(wkv0), C)
    o_a = _rwkv_post(yt, g_, bon, W["lnx_w"], W["lnx_b"], m2)

    o_b = moba_fn(q, k, v)

    x1 = _mixout(x, o_a, o_b, pg, gt_m, W["g_post_mix"], W["w_out_a"], W["w_out_b"], gb, min(tm, 256))
    y = _ffn(x1, W["g_pre_ffn"], sc_f, sh_f, gt_f, W["g_post_ffn"], W["w_ffn_in"], W["w_ffn_out"], gb, tm, 512)
    return y, last[:, 0, :], _unpair_state(s_fin, aw // A_HEAD_DIM), k, v


def kernel(x_prompt, x_sample, cache_k, cache_v, page_table, state_wkv, state_shift, c_prompt, c_sample, w_ada, b_ada, g_pre_mix, g_post_mix, g_pre_ffn, g_post_ffn, w_in, mu_shift, w0, w2, a0, a2, g2, k_k, k_a, r_k, lnx_w, lnx_b, w_out, w_ffn_in, w_ffn_out):
    depth = w_in.shape[0]
    assert depth == 1
    l = 0
    Bp, Tp, D = x_prompt.shape
    Bs, Ts, _ = x_sample.shape
    aw = w0.shape[-1]
    bw = D - aw
    ac = mu_shift.shape[-1]
    nh_a = aw // A_HEAD_DIM
    nh_b = bw // B_HEAD_DIM
    assert Bp == 1

    nc = Bp + Bs
    ncp = -(-nc // SUBLANES) * SUBLANES
    c_all = jnp.concatenate([c_prompt, c_sample, jnp.zeros((ncp - nc, D), F32)], axis=0)
    mod = _ada(c_all, w_ada[l], b_ada[l]).reshape(ncp, 6, D)
    mod_p, mod_s = mod[:Bp], mod[Bp:nc]

    wi = w_in[l]
    zeros_wa = jnp.zeros((DECAY_LORA, aw), F32)
    wwa = jnp.concatenate([jnp.concatenate([w2[l], zeros_wa], axis=1),
                           jnp.concatenate([zeros_wa, a2[l]], axis=1)], axis=0).astype(BF16)
    W = dict(
        m2=_block_ones(),
        g_pre_mix=g_pre_mix[l].reshape(1, 1, D), g_post_mix=g_post_mix[l].reshape(1, 1, D),
        g_pre_ffn=g_pre_ffn[l].reshape(1, 1, D), g_post_ffn=g_post_ffn[l].reshape(1, 1, D),
        w_in_a=wi[:, :ac].astype(BF16),
        w_in_q=wi[:, ac:ac + bw].astype(BF16),
        w_in_k=wi[:, ac + bw:ac + 2 * bw].astype(BF16),
        w_in_v=wi[:, ac + 2 * bw:ac + 3 * bw].astype(BF16),
        w_in_g=wi[:, ac + 3 * bw:].astype(BF16),
        mu=mu_shift[l].reshape(1, ac), w0=w0[l].reshape(1, aw), a0=a0[l].reshape(1, aw),
        k_k=k_k[l].reshape(1, aw), k_a=k_a[l].reshape(1, aw), r_k=r_k[l].reshape(1, aw),
        wwa=wwa, g2=g2[l].astype(BF16),
        lnx_w=lnx_w[l].reshape(1, aw), lnx_b=lnx_b[l].reshape(1, aw),
        w_out_a=w_out[l][:aw].astype(BF16), w_out_b=w_out[l][aw:].astype(BF16),
        w_ffn_in=w_ffn_in[l].astype(BF16), w_ffn_out=w_ffn_out[l].astype(BF16),
    )

    def moba_prompt_fn(q, k, v):
        kb, vt, km = _moba_prep(k[0], v[0])
        return _moba_prompt(q[0], kb, vt, km[:, 0, :])[None]

    def moba_sample_fn(q, k, v):
        o = _moba_sample(q, k, v, cache_k, cache_v, page_table)
        return o.reshape(Bs, nh_b, Ts, B_HEAD_DIM).transpose(0, 2, 1, 3).reshape(Bs, Ts, bw)

    zero_shift = jnp.zeros((Bp, ac), F32)
    zero_wkv = jnp.zeros((Bp, nh_a, A_HEAD_DIM, A_HEAD_DIM), F32)
    yp, shp, wkvp, kp, vp = _trunk(x_prompt, mod_p, zero_shift, zero_wkv, moba_prompt_fn, W, 1, 512)
    ys, shs, wkvs, ksm, vsm = _trunk(x_sample, mod_s, state_shift[l], state_wkv[l], moba_sample_fn, W, Bs, Ts)

    hd = lambda z, B, T: z.reshape(1, B, T, nh_b, B_HEAD_DIM)
    return (yp, ys, hd(kp, Bp, Tp), hd(vp, Bp, Tp), wkvp[None], shp[None],
            hd(ksm, Bs, Ts), hd(vsm, Bs, Ts), wkvs[None], shs[None])
```

```python
import functools

import numpy as np
import jax
import jax.numpy as jnp
from jax import lax
from jax.experimental import pallas as pl
from jax.experimental.pallas import tpu as pltpu

F32 = jnp.float32
BF16 = jnp.bfloat16

A_HEAD_DIM = 64
DECAY_LORA = 64
AAA_LORA = 64
GATE_LORA = 128
B_HEAD_DIM = 128
MOBA_BLOCK = 256
MOBA_TOPK = 3
PAGE_SIZE = 128
LNX_EPS = 64e-5
RMS_EPS = 1e-6
NEG = -1e30

LANES = 128
SUBLANES = 8
VMEM_LIMIT = 56 * 1024 * 1024
VMEM_LIMIT_ATTN = 60 * 1024 * 1024


def _cparams(sem):
    return pltpu.CompilerParams(dimension_semantics=sem, vmem_limit_bytes=VMEM_LIMIT)


def _split_bf16(x):
    hi = x.astype(BF16)
    lo = (x - hi.astype(F32)).astype(BF16)
    return hi, lo


def _seg_sum128(x, m2):
    hi, lo = _split_bf16(x)
    return jnp.dot(jnp.concatenate([hi, lo], axis=1), m2, preferred_element_type=F32)


def _seg_sum(x, m2):
    n = x.shape[1] // LANES
    return jnp.concatenate([_seg_sum128(x[:, c * LANES:(c + 1) * LANES], m2) for c in range(n)], axis=1)


def _block_ones_np():
    i = np.arange(LANES)
    return (i[:, None] // A_HEAD_DIM == i[None, :] // A_HEAD_DIM).astype(np.float32)


def _block_ones():
    m = _block_ones_np()
    return jnp.asarray(np.concatenate([m, m], axis=0), dtype=BF16)


def _ada_kernel(c_ref, w_ref, b_ref, o_ref):
    c = c_ref[...]
    s = (c * jax.nn.sigmoid(c)).astype(BF16)
    o_ref[...] = jnp.dot(s, w_ref[...].astype(BF16), preferred_element_type=F32) + b_ref[...]


def _ada(c, w_ada, b_ada):
    m, d = c.shape
    n = w_ada.shape[1]
    tn = 1024
    return pl.pallas_call(
        _ada_kernel,
        grid=(n // tn,),
        in_specs=[pl.BlockSpec((m, d), lambda j: (0, 0)),
                  pl.BlockSpec((d, tn), lambda j: (0, j)),
                  pl.BlockSpec((1, tn), lambda j: (0, j))],
        out_specs=pl.BlockSpec((m, tn), lambda j: (0, j)),
        out_shape=jax.ShapeDtypeStruct((m, n), F32),
        compiler_params=_cparams(("arbitrary",)),
        name="ada_mod",
    )(c, w_ada, b_ada.reshape(1, n))


def _norm_mod(x, g, sc, sh):
    ms = jnp.mean(x * x, axis=-1, keepdims=True)
    y = x * lax.rsqrt(ms + RMS_EPS) * g
    return y * (1.0 + sc) + sh


def _norm_mod_kernel(x_ref, g_ref, sc_ref, sh_ref, h_ref):
    h_ref[...] = _norm_mod(x_ref[...], g_ref[...], sc_ref[...], sh_ref[...]).astype(h_ref.dtype)


def _norm_mod_rows(x, g, sc, sh, gb, tm):
    G, T, D = x.shape
    row = pl.BlockSpec((gb, tm, D), lambda a, i: (a, i, 0))
    mod = pl.BlockSpec((gb, 1, D), lambda a, i: (a, 0, 0))
    return pl.pallas_call(
        _norm_mod_kernel,
        grid=(G // gb, T // tm),
        in_specs=[row, pl.BlockSpec((1, 1, D), lambda a, i: (0, 0, 0)), mod, mod],
        out_specs=row,
        out_shape=jax.ShapeDtypeStruct((G, T, D), BF16),
        compiler_params=_cparams(("arbitrary", "arbitrary")),
        name="norm_mod",
    )(x, g, sc, sh)


def _matmul_kernel(h_ref, w_ref, o_ref):
    gb, tm, d = h_ref.shape
    o = jnp.dot(h_ref[...].reshape(gb * tm, d), w_ref[...], preferred_element_type=F32)
    o_ref[...] = o.reshape(o_ref.shape)


def _matmul(h, w, gb, tm, tn):
    G, T, D = h.shape
    N = w.shape[1]
    return pl.pallas_call(
        _matmul_kernel,
        grid=(N // tn, G // gb, T // tm),
        in_specs=[pl.BlockSpec((gb, tm, D), lambda j, a, i: (a, i, 0)),
                  pl.BlockSpec((D, tn), lambda j, a, i: (0, j))],
        out_specs=pl.BlockSpec((gb, tm, tn), lambda j, a, i: (a, i, j)),
        out_shape=jax.ShapeDtypeStruct((G, T, N), F32),
        compiler_params=_cparams(("arbitrary", "arbitrary", "arbitrary")),
        name="proj_matmul",
    )(h, w)


def _softplus(z):
    return jnp.maximum(z, 0.0) + jnp.log(1.0 + jnp.exp(-jnp.abs(z)))


def _prep_kernel(pa_ref, sh0_ref, mu_ref, w0_ref, a0_ref, kk_ref, ka_ref, rk_ref, wwa_ref, g2_ref, m2_ref,
                 r_o, w_o, k_o, v_o, a_o, b_o, c_o, g_o, bon_o, last_o, carry_ref):
    tm = pa_ref.shape[1]
    aw = r_o.shape[2]

    @pl.when(pl.program_id(1) == 0)
    def _():
        carry_ref[...] = sh0_ref[0]

    pa = pa_ref[0]
    rolled = pltpu.roll(pa, 1, axis=0)
    row = lax.broadcasted_iota(jnp.int32, (tm, 1), 0)
    prev = jnp.where(row == 0, carry_ref[...], rolled)
    last = pa[tm - 1:tm, :]
    carry_ref[...] = last
    last_o[0] = last

    ps = pa + (prev - pa) * mu_ref[...]
    r = ps[:, 0:aw]
    k = ps[:, aw:2 * aw]
    v = ps[:, 2 * aw:3 * aw]
    o3 = 3 * aw
    x_wa = ps[:, o3:o3 + DECAY_LORA + AAA_LORA]
    gl = ps[:, o3 + DECAY_LORA + AAA_LORA:]
    lane = lax.broadcasted_iota(jnp.int32, x_wa.shape, 1)
    z = jnp.where(lane < DECAY_LORA, jnp.tanh(x_wa), x_wa).astype(BF16)
    wa = jnp.dot(z, wwa_ref[...], preferred_element_type=F32)
    lw = wa[:, :aw]
    la = wa[:, aw:]
    g = jnp.dot(jax.nn.sigmoid(gl).astype(BF16), g2_ref[...], preferred_element_type=F32)

    w_log = -_softplus(-(w0_ref[...] + lw)) - 0.5
    decay = jnp.exp(-jnp.exp(w_log))
    a = jax.nn.sigmoid(a0_ref[...] + la)
    m2 = m2_ref[...]
    kk = k * kk_ref[...]
    kk = kk * lax.rsqrt(jnp.maximum(_seg_sum(kk * kk, m2), 1e-24))
    k_mod = k * (1.0 + (a - 1.0) * ka_ref[...])
    bonus = _seg_sum(r * k_mod * rk_ref[...], m2) * v

    a_vec = -kk
    b_vec = kk * a
    a_next = pltpu.roll(a_vec, tm - 1, axis=0)
    r_o[0] = r
    w_o[0] = decay
    k_o[0] = k_mod
    v_o[0] = v
    a_o[0] = a_vec
    b_o[0] = b_vec
    c_o[0] = _seg_sum(b_vec * a_next, m2)
    g_o[0] = g
    bon_o[0] = bonus


def _rwkv_prep(pa, shift0, mu, w0, a0, k_k, k_a, r_k, wwa, g2, m2, tm):
    G, T, AC = pa.shape
    aw = w0.shape[-1]
    vec = lambda n: pl.BlockSpec((1, n), lambda a, i: (0, 0))
    full = lambda s: pl.BlockSpec(s, lambda a, i: (0, 0))
    big = pl.BlockSpec((1, tm, aw), lambda a, i: (a, i, 0))
    assert tm % 2 == 0
    outs = [jax.ShapeDtypeStruct((G, T, aw), F32)] * 9 + [jax.ShapeDtypeStruct((G, 1, AC), F32)]
    return pl.pallas_call(
        _prep_kernel,
        grid=(G, T // tm),
        in_specs=[pl.BlockSpec((1, tm, AC), lambda a, i: (a, i, 0)),
                  pl.BlockSpec((1, 1, AC), lambda a, i: (a, 0, 0)),
                  vec(AC), vec(aw), vec(aw), vec(aw), vec(aw), vec(aw),
                  full(wwa.shape), full(g2.shape), full(m2.shape)],
        out_specs=[big] * 9 + [pl.BlockSpec((1, 1, AC), lambda a, i: (a, 0, 0))],
        out_shape=outs,
        scratch_shapes=[pltpu.VMEM((1, AC), F32)],
        compiler_params=_cparams(("arbitrary", "arbitrary")),
        name="rwkv_prep",
    )(pa, shift0, mu, w0, a0, k_k, k_a, r_k, wwa, g2, m2)


def _half_lane_unzip(xt):
    lo_half = lax.broadcasted_iota(jnp.int32, (A_HEAD_DIM, LANES), 1) < A_HEAD_DIM
    top = xt[0:A_HEAD_DIM]
    bot = xt[A_HEAD_DIM:2 * A_HEAD_DIM]
    return (jnp.where(lo_half, top, pltpu.roll(bot, A_HEAD_DIM, axis=1)),
            jnp.where(lo_half, pltpu.roll(top, A_HEAD_DIM, axis=1), bot))


def _scan_kernel(r_ref, w_ref, k_ref, a_ref, b_ref, c_ref, v2_ref, s0_ref, my_ref, y_ref, sf_ref, s_ref,
                 vt_ref, q_ref, vb_ref):
    C = r_ref.shape[1]
    npair = r_ref.shape[2]
    rows = npair * A_HEAD_DIM

    @pl.when(pl.program_id(1) == 0)
    def _():
        s_ref[...] = s0_ref[0]

    v2 = v2_ref[0]
    if C < A_HEAD_DIM:
        v2 = jnp.concatenate([v2, jnp.zeros((A_HEAD_DIM - C, v2.shape[1]), F32)], axis=0)
    for pp in range(npair // 2):
        x = jnp.concatenate([v2[:, (2 * pp) * LANES:(2 * pp + 1) * LANES],
                             v2[:, (2 * pp + 1) * LANES:(2 * pp + 2) * LANES]], axis=0)
        for j, o in enumerate(_half_lane_unzip(x.T)):
            p = 2 * pp + j
            vt_ref[p * A_HEAD_DIM:(p + 1) * A_HEAD_DIM, :] = o

    half0 = (lax.broadcasted_iota(jnp.int32, (rows, LANES), 1) // A_HEAD_DIM) * A_HEAD_DIM

    def value_columns(j):
        vt = vt_ref[...]
        return jnp.concatenate(
            [jnp.take_along_axis(vt, half0 + 2 * j, axis=1, mode="promise_in_bounds"),
             jnp.take_along_axis(vt, half0 + (2 * j + 1), axis=1, mode="promise_in_bounds")], axis=1)

    y_ref[...] = jnp.zeros(y_ref.shape, F32)
    q_ref[...] = jnp.zeros(q_ref.shape, BF16)
    my = my_ref[...]
    tlane = lax.broadcasted_iota(jnp.int32, (rows, LANES), 1) % A_HEAD_DIM

    def bc(ref, t):
        return jnp.concatenate(
            [jnp.broadcast_to(ref[0, t, p:p + 1, :], (A_HEAD_DIM, LANES)) for p in range(npair)], axis=0)

    def put_y(j):
        yb = jnp.dot(q_ref[...], my, preferred_element_type=F32)
        y_ref[0, 0] = jnp.where(tlane == 2 * j, yb[:, :LANES],
                                jnp.where(tlane == 2 * j + 1, yb[:, LANES:], y_ref[0, 0]))

    def two_steps(j, carry):
        t0 = 2 * j
        t1 = t0 + 1
        vb = vb_ref[...]
        S0 = s_ref[...]
        X = S0 * bc(w_ref, t0) + vb[:, :LANES] * bc(k_ref, t0)
        lhs = jnp.concatenate([(S0 * bc(a_ref, t0)).astype(BF16), (X * bc(a_ref, t1)).astype(BF16)], axis=1)
        out = jnp.dot(lhs, my, preferred_element_type=F32)
        put_y(j - 1)
        vb_next = value_columns(jnp.minimum(j + 1, C // 2 - 1))
        sa0 = out[:, :LANES]
        sa1 = out[:, LANES:] + sa0 * bc(c_ref, t0)
        S1 = X + sa0 * bc(b_ref, t0)
        S2 = S1 * bc(w_ref, t1) + sa1 * bc(b_ref, t1) + vb[:, LANES:] * bc(k_ref, t1)
        s_ref[...] = S2
        q_ref[...] = jnp.concatenate([(S1 * bc(r_ref, t0)).astype(BF16), (S2 * bc(r_ref, t1)).astype(BF16)], axis=1)
        vb_ref[...] = vb_next
        return carry

    vb_ref[...] = value_columns(0)
    lax.fori_loop(0, C // 2, two_steps, 0, unroll=2)
    put_y(C // 2 - 1)
    sf_ref[0] = s_ref[...]


def _block_ones_pair():
    m = _block_ones_np()
    z = np.zeros_like(m)
    return jnp.asarray(np.block([[m, z], [z, m]]), dtype=BF16)


def _wkv_scan(r, w, k, a, b, c, v2, s0, C):
    G, T, npair, _ = r.shape
    rows = npair * A_HEAD_DIM
    assert C % 2 == 0
    my = _block_ones_pair()
    seq = pl.BlockSpec((1, C, npair, LANES), lambda g, c: (g, c, 0, 0))
    st = pl.BlockSpec((1, rows, LANES), lambda g, c: (g, 0, 0))
    return pl.pallas_call(
        _scan_kernel,
        grid=(G, T // C),
        in_specs=[seq] * 6 + [pl.BlockSpec((1, C, v2.shape[2]), lambda g, c: (g, c, 0)), st,
                              pl.BlockSpec(my.shape, lambda g, c: (0, 0))],
        out_specs=[pl.BlockSpec((1, 1, rows, LANES), lambda g, c: (g, c, 0, 0)), st],
        out_shape=[jax.ShapeDtypeStruct((G, T // C, rows, LANES), F32),
                   jax.ShapeDtypeStruct((G, rows, LANES), F32)],
        scratch_shapes=[pltpu.VMEM((rows, LANES), F32), pltpu.VMEM((rows, LANES), F32),
                        pltpu.VMEM((rows, 2 * LANES), BF16), pltpu.VMEM((rows, 2 * LANES), F32)],
        compiler_params=_cparams(("arbitrary", "arbitrary")),
        name="wkv_scan",
    )(r, w, k, a, b, c, v2, s0, my)


def _post_kernel(yt_ref, g_ref, bon_ref, lw_ref, lb_ref, m2_ref, o_ref):
    C = o_ref.shape[1]
    rows = yt_ref.shape[2]
    chunks = []
    for pp in range(rows // LANES):
        rt = yt_ref[0, 0, pp * LANES:(pp + 1) * LANES, :].T
        chunks.extend(o[:C] for o in _half_lane_unzip(rt))
    y = jnp.concatenate(chunks, axis=1)
    m2 = m2_ref[...]
    inv = 1.0 / A_HEAD_DIM
    mean = _seg_sum(y, m2) * inv
    yc = y - mean
    var = _seg_sum(yc * yc, m2) * inv
    yn = yc * lax.rsqrt(var + LNX_EPS) * lw_ref[...] + lb_ref[...]
    o_ref[0] = ((yn + bon_ref[0]) * g_ref[0]).astype(o_ref.dtype)


def _rwkv_post(yt, g, bon, lnx_w, lnx_b, m2):
    G, NC, rows, _ = yt.shape
    T, aw = g.shape[1], g.shape[2]
    C = T // NC
    big = pl.BlockSpec((1, C, aw), lambda a, c: (a, c, 0))
    vec = pl.BlockSpec((1, aw), lambda a, c: (0, 0))
    return pl.pallas_call(
        _post_kernel,
        grid=(G, NC),
        in_specs=[pl.BlockSpec((1, 1, rows, LANES), lambda a, c: (a, c, 0, 0)), big, big, vec, vec,
                  pl.BlockSpec(m2.shape, lambda a, c: (0, 0))],
        out_specs=big,
        out_shape=jax.ShapeDtypeStruct((G, T, aw), F32),
        compiler_params=_cparams(("arbitrary", "arbitrary")),
        name="rwkv_post",
    )(yt, g, bon, lnx_w, lnx_b, m2)


def _moba_prep_kernel(k_ref, v_ref, kb_ref, vt_ref, km_ref):
    k = k_ref[...]
    kb_ref[...] = k.astype(BF16)
    km_ref[0] = jnp.mean(k, axis=0, keepdims=True)
    vt_ref[...] = v_ref[...].T.astype(BF16)


def _moba_prep(k, v):
    T, W = k.shape
    nb = T // MOBA_BLOCK
    blk = pl.BlockSpec((MOBA_BLOCK, W), lambda n: (n, 0))
    return pl.pallas_call(
        _moba_prep_kernel,
        grid=(nb,),
        in_specs=[blk, blk],
        out_specs=[blk, pl.BlockSpec((W, MOBA_BLOCK), lambda n: (0, n)),
                   pl.BlockSpec((1, 1, W), lambda n: (n, 0, 0))],
        out_shape=[jax.ShapeDtypeStruct((T, W), BF16), jax.ShapeDtypeStruct((W, T), BF16),
                   jax.ShapeDtypeStruct((nb, 1, W), F32)],
        compiler_params=_cparams(("arbitrary",)),
        name="moba_prep",
    )(k, v)


def _select_topk(gate, valid, axis):
    idx = lax.broadcasted_iota(jnp.int32, gate.shape, axis)
    g = jnp.where(valid, gate, NEG)
    sel = jnp.zeros(gate.shape, jnp.bool_)
    for _ in range(MOBA_TOPK):
        m = jnp.max(g, axis=axis, keepdims=True)
        first = jnp.min(jnp.where(g == m, idx, jnp.int32(2 ** 30)), axis=axis, keepdims=True)
        pick = idx == first
        sel = jnp.logical_or(sel, pick)
        g = jnp.where(pick, -jnp.inf, g)
    return jnp.logical_and(sel, valid)


PROMPT_HEADS_PER_STEP = 2
PROMPT_GROUP = 8
LOG2E = 1.4426950408889634


def _moba_prompt_kernel(q_ref, k_ref, vt_ref, km_ref, o_ref, sel_ref, s_ref):
    i = pl.program_id(1)
    tq = q_ref.shape[0]
    nb = km_ref.shape[0]
    dh = B_HEAD_DIM
    nt = (((1,), (1,)), ((), ()))
    heads = range(PROMPT_HEADS_PER_STEP)
    blk = lax.broadcasted_iota(jnp.int32, (nb, tq), 0)
    key = lax.broadcasted_iota(jnp.int32, (MOBA_BLOCK, tq), 0)
    qi = lax.broadcasted_iota(jnp.int32, (MOBA_BLOCK, tq), 1)
    start = pl.multiple_of(i * MOBA_BLOCK, MOBA_BLOCK)

    qbs = []
    for u in heads:
        sl = slice(u * dh, (u + 1) * dh)
        q = q_ref[:, sl]
        qb = (q * (dh ** -0.5 * LOG2E)).astype(BF16)
        qbs.append(qb)
        gate_t = lax.dot_general(km_ref[:, sl], q, nt, precision=lax.Precision.HIGHEST,
                                 preferred_element_type=F32)
        sel_ref[u, 0:nb, :] = _select_topk(gate_t, blk < i, 0).astype(F32)
        sel_ref[u, nb:, :] = jnp.zeros((sel_ref.shape[1] - nb, tq), F32)

    def scores(n, u):
        st = pl.multiple_of(n * MOBA_BLOCK, MOBA_BLOCK)
        return lax.dot_general(k_ref[pl.ds(st, MOBA_BLOCK), u * dh:(u + 1) * dh], qbs[u], nt,
                               preferred_element_type=F32)

    def selected(n, u):
        return sel_ref[u, pl.ds(n, 1), :] > 0.5

    gk = PROMPT_GROUP * MOBA_BLOCK
    n_iter = (i + PROMPT_GROUP - 1) // PROMPT_GROUP

    def pass1(g, ms):
        st = pl.multiple_of(g * gk, gk)
        out = []
        for u in heads:
            m = ms[u]
            s_g = lax.dot_general(k_ref[pl.ds(st, gk), u * dh:(u + 1) * dh], qbs[u], nt,
                                  preferred_element_type=F32)
            s_ref[u, pl.ds(st, gk), :] = s_g
            for c in range(PROMPT_GROUP):
                mb = jnp.max(s_g[c * MOBA_BLOCK:(c + 1) * MOBA_BLOCK], axis=0, keepdims=True)
                m = jnp.maximum(m, jnp.where(selected(g * PROMPT_GROUP + c, u), mb, NEG))
            out.append(m)
        return tuple(out)

    ms = lax.fori_loop(0, n_iter, pass1, tuple(jnp.full((1, tq), NEG, F32) for _ in heads))

    init = []
    for u in heads:
        sl = slice(u * dh, (u + 1) * dh)
        s_t = jnp.where(key <= qi, scores(i, u), NEG)
        m = jnp.maximum(ms[u], jnp.max(s_t, axis=0, keepdims=True))
        p = jnp.exp2(s_t - m)
        l = jnp.sum(p, axis=0, keepdims=True)
        acc = jnp.dot(vt_ref[sl, pl.ds(start, MOBA_BLOCK)], p.astype(BF16), preferred_element_type=F32)
        init.append((m, l, acc))
    ms = tuple(m for m, _, _ in init)

    def pass2(g, carry):
        st = pl.multiple_of(g * gk, gk)
        out = []
        for u in heads:
            sl = slice(u * dh, (u + 1) * dh)
            l, acc = carry[u]
            ps = []
            for c in range(PROMPT_GROUP):
                n = g * PROMPT_GROUP + c
                s_t = s_ref[u, pl.ds(pl.multiple_of(n * MOBA_BLOCK, MOBA_BLOCK), MOBA_BLOCK), :]
                p = jnp.exp2(s_t - jnp.where(selected(n, u), ms[u], -NEG))
                l = l + jnp.sum(p, axis=0, keepdims=True)
                ps.append(p.astype(BF16))
            acc = acc + jnp.dot(vt_ref[sl, pl.ds(st, gk)], jnp.concatenate(ps, axis=0),
                                preferred_element_type=F32)
            out.append((l, acc))
        return tuple(out)

    fin = lax.fori_loop(0, n_iter, pass2, tuple((l, acc) for _, l, acc in init))
    for u in heads:
        l, acc = fin[u]
        o_ref[:, u * dh:(u + 1) * dh] = (acc / l).T


def _moba_prompt(q, kb, vt, kmean):
    T, W = q.shape
    hw = PROMPT_HEADS_PER_STEP * B_HEAD_DIM
    nb = T // MOBA_BLOCK
    tp = T if nb % PROMPT_GROUP == 0 else T + (PROMPT_GROUP - 1) * MOBA_BLOCK
    kb = jnp.pad(kb, ((0, tp - T), (0, 0)))
    vt = jnp.pad(vt, ((0, 0), (0, tp - T)))
    once = pl.Buffered(1)
    return pl.pallas_call(
        _moba_prompt_kernel,
        grid=(W // hw, nb),
        in_specs=[pl.BlockSpec((MOBA_BLOCK, hw), lambda h, i: (i, h)),
                  pl.BlockSpec((tp, hw), lambda h, i: (0, h), pipeline_mode=once),
                  pl.BlockSpec((hw, tp), lambda h, i: (h, 0), pipeline_mode=once),
                  pl.BlockSpec((nb, hw), lambda h, i: (0, h))],
        out_specs=pl.BlockSpec((MOBA_BLOCK, hw), lambda h, i: (i, h)),
        out_shape=jax.ShapeDtypeStruct((T, W), F32),
        scratch_shapes=[pltpu.VMEM((PROMPT_HEADS_PER_STEP, nb + SUBLANES, MOBA_BLOCK), F32),
                        pltpu.VMEM((PROMPT_HEADS_PER_STEP, tp, MOBA_BLOCK), F32)],
        compiler_params=pltpu.CompilerParams(dimension_semantics=("arbitrary", "arbitrary"),
                                             vmem_limit_bytes=VMEM_LIMIT_ATTN),
        name="moba_prompt",
    )(q, kb, vt, kmean)


SAMPLE_PAGES_PER_STEP = 8


def _moba_sample_kernel(pt_ref, q_ref, kn_ref, vn_ref, *rest):
    P = SAMPLE_PAGES_PER_STEP
    kp = rest[:P]
    vp = rest[P:2 * P]
    o_ref = rest[2 * P]
    km_ref, m_ref, l_ref, oacc_ref = rest[2 * P + 1:]
    s = pl.program_id(1)
    ns = pl.num_programs(1)
    tq, W = q_ref.shape[1], q_ref.shape[2]
    dh = B_HEAD_DIM
    nh = W // dh
    hq = nh * tq
    nbp = km_ref.shape[1]
    ppb = MOBA_BLOCK // PAGE_SIZE
    bps = P // ppb
    nt = (((1,), (1,)), ((), ()))

    def by_head(x):
        return jnp.concatenate([x[:, h * dh:(h + 1) * dh] for h in range(nh)], axis=0)

    qall = by_head(q_ref[0])
    qall_b = (qall * (dh ** -0.5)).astype(BF16)

    def partial(k_b, v_b, mask):
        sc = jnp.where(mask, lax.dot_general(qall_b, k_b, nt, preferred_element_type=F32), NEG)
        m = jnp.max(sc, axis=1, keepdims=True)
        p = jnp.exp(sc - m)
        l = jnp.sum(p, axis=1, keepdims=True)
        o = jnp.dot(p.astype(BF16), v_b, preferred_element_type=F32)
        return jnp.broadcast_to(m, (hq, dh)), jnp.broadcast_to(l, (hq, dh)), o

    nflat = MOBA_BLOCK * nh
    rowh = lax.broadcasted_iota(jnp.int32, (hq, nflat), 0) // tq
    colh = lax.broadcasted_iota(jnp.int32, (hq, nflat), 1) % nh
    same_head = rowh == colh
    for j in range(bps):
        kpages = [kp[j * ppb + u][0, 0] for u in range(ppb)]
        vpages = [vp[j * ppb + u][0, 0] for u in range(ppb)]
        n = s * bps + j
        kmean = sum(jnp.sum(pg, axis=0) for pg in kpages) * (1.0 / MOBA_BLOCK)
        for h in range(nh):
            km_ref[h, pl.ds(n, 1), :] = kmean[h:h + 1, :]
        kflat = jnp.concatenate([pg.reshape(PAGE_SIZE * nh, dh) for pg in kpages], axis=0).astype(BF16)
        vflat = jnp.concatenate([pg.reshape(PAGE_SIZE * nh, dh) for pg in vpages], axis=0).astype(BF16)
        m, l, o = partial(kflat, vflat, same_head)
        m_ref[n] = m
        l_ref[n] = l
        oacc_ref[n] = o

    @pl.when(s == ns - 1)
    def _():
        pad = jnp.zeros((LANES - hq, dh), F32)
        kn = jnp.concatenate([by_head(kn_ref[0]), pad], axis=0).astype(BF16)
        vn = jnp.concatenate([by_head(vn_ref[0]), pad], axis=0).astype(BF16)
        col = lax.broadcasted_iota(jnp.int32, (hq, LANES), 1)
        row = lax.broadcasted_iota(jnp.int32, (hq, LANES), 0)
        own_mask = jnp.logical_and(col // tq == row // tq, col % tq <= row % tq)
        m_own, l_own, o_own = partial(kn, vn, own_mask)

        gate = jnp.concatenate(
            [lax.dot_general(qall[h * tq:(h + 1) * tq], km_ref[h], nt, precision=lax.Precision.HIGHEST,
                             preferred_element_type=F32) for h in range(nh)], axis=0)
        sel = _select_topk(gate, jnp.ones(gate.shape, jnp.bool_), 1).astype(F32)
        mx = m_own
        for n in range(nbp):
            sn = sel[:, n:n + 1] > 0.5
            mx = jnp.where(sn, jnp.maximum(mx, m_ref[n]), mx)
        w_own = jnp.exp(m_own - mx)
        lsum = w_own * l_own
        osum = w_own * o_own
        for n in range(nbp):
            sn = sel[:, n:n + 1] > 0.5
            wn = jnp.where(sn, jnp.exp(m_ref[n] - mx), 0.0)
            lsum = lsum + wn * l_ref[n]
            osum = osum + wn * oacc_ref[n]
        o_ref[0] = osum / lsum


def _moba_sample(q, kn, vn, cache_k, cache_v, page_table):
    B, tq, W = q.shape
    nh = W // B_HEAD_DIM
    hq = nh * tq
    assert hq <= LANES
    n_pages = page_table.shape[1]
    P = SAMPLE_PAGES_PER_STEP
    nbp = n_pages * PAGE_SIZE // MOBA_BLOCK
    small = pl.BlockSpec((1, tq, W), lambda b, s, pt: (b, 0, 0))

    def page_spec(j):
        return pl.BlockSpec((1, 1, PAGE_SIZE, nh, B_HEAD_DIM), lambda b, s, pt: (0, pt[b, s * P + j], 0, 0, 0))

    grid_spec = pltpu.PrefetchScalarGridSpec(
        num_scalar_prefetch=1,
        grid=(B, n_pages // P),
        in_specs=[small, small, small] + [page_spec(j) for j in range(P)] * 2,
        out_specs=pl.BlockSpec((1, hq, B_HEAD_DIM), lambda b, s, pt: (b, 0, 0)),
        scratch_shapes=[pltpu.VMEM((nh, nbp, B_HEAD_DIM), F32),
                        pltpu.VMEM((nbp, hq, B_HEAD_DIM), F32), pltpu.VMEM((nbp, hq, B_HEAD_DIM), F32),
                        pltpu.VMEM((nbp, hq, B_HEAD_DIM), F32)],
    )
    return pl.pallas_call(
        _moba_sample_kernel,
        grid_spec=grid_spec,
        out_shape=jax.ShapeDtypeStruct((B, hq, B_HEAD_DIM), F32),
        compiler_params=_cparams(("arbitrary", "arbitrary")),
        name="moba_sample",
    )(page_table, q, kn, vn, *([cache_k] * P), *([cache_v] * P))


def _mixout_kernel(x_ref, oa_ref, ob_ref, pg_ref, gt_ref, gp_ref, wa_ref, wb_ref, o_ref):
    gb, tm, d = x_ref.shape
    aw = oa_ref.shape[2]
    pg = pg_ref[...]
    ma = (oa_ref[...] * jax.nn.sigmoid(pg[..., :aw])).reshape(gb * tm, aw).astype(BF16)
    mb = (ob_ref[...] * jax.nn.sigmoid(pg[..., aw:])).reshape(gb * tm, pg.shape[2] - aw).astype(BF16)
    f = jnp.dot(ma, wa_ref[...], preferred_element_type=F32) + jnp.dot(mb, wb_ref[...], preferred_element_type=F32)
    y = f * lax.rsqrt(jnp.mean(f * f, axis=-1, keepdims=True) + RMS_EPS)
    y = y.reshape(gb, tm, d) * gp_ref[...]
    o_ref[...] = x_ref[...] + gt_ref[...] * y


def _mixout(x, oa, ob, pg, gt, gpost, wa, wb, gb, tm):
    G, T, D = x.shape
    aw, bw = oa.shape[2], ob.shape[2]
    row = lambda n: pl.BlockSpec((gb, tm, n), lambda a, i: (a, i, 0))
    return pl.pallas_call(
        _mixout_kernel,
        grid=(G // gb, T // tm),
        in_specs=[row(D), row(aw), row(bw), row(aw + bw),
                  pl.BlockSpec((gb, 1, D), lambda a, i: (a, 0, 0)),
                  pl.BlockSpec((1, 1, D), lambda a, i: (0, 0, 0)),
                  pl.BlockSpec(wa.shape, lambda a, i: (0, 0)),
                  pl.BlockSpec(wb.shape, lambda a, i: (0, 0))],
        out_specs=row(D),
        out_shape=jax.ShapeDtypeStruct((G, T, D), F32),
        compiler_params=_cparams(("arbitrary", "arbitrary")),
        name="mix_out",
    )(x, oa, ob, pg, gt, gpost, wa, wb)


def _ffn_kernel(x_ref, gpre_ref, sc_ref, sh_ref, gt_ref, gpost_ref, wg_ref, wu_ref, wo_ref, o_ref, h_ref, acc_ref):
    gb, tm, d = x_ref.shape
    j = pl.program_id(2)

    @pl.when(j == 0)
    def _():
        h = _norm_mod(x_ref[...], gpre_ref[...], sc_ref[...], sh_ref[...])
        h_ref[...] = h.reshape(gb * tm, d).astype(BF16)
        acc_ref[...] = jnp.zeros(acc_ref.shape, F32)

    h = h_ref[...]
    g = jnp.dot(h, wg_ref[...], preferred_element_type=F32)
    u = jnp.dot(h, wu_ref[...], preferred_element_type=F32)
    act = (g * jax.nn.sigmoid(g) * u).astype(BF16)
    acc_ref[...] += jnp.dot(act, wo_ref[...], preferred_element_type=F32)

    @pl.when(j == pl.num_programs(2) - 1)
    def _():
        f = acc_ref[...]
        y = f * lax.rsqrt(jnp.mean(f * f, axis=-1, keepdims=True) + RMS_EPS)
        y = y.reshape(gb, tm, d) * gpost_ref[...]
        o_ref[...] = x_ref[...] + gt_ref[...] * y


def _ffn(x, gpre, sc, sh, gt, gpost, w_in, w_out, gb, tm, tf):
    G, T, D = x.shape
    dff = w_out.shape[0]
    nj = dff // tf
    row = pl.BlockSpec((gb, tm, D), lambda a, i, j: (a, i, 0))
    mod = pl.BlockSpec((gb, 1, D), lambda a, i, j: (a, 0, 0))
    gain = pl.BlockSpec((1, 1, D), lambda a, i, j: (0, 0, 0))
    return pl.pallas_call(
        _ffn_kernel,
        grid=(G // gb, T // tm, nj),
        in_specs=[row, gain, mod, mod, mod, gain,
                  pl.BlockSpec((D, tf), lambda a, i, j: (0, j)),
                  pl.BlockSpec((D, tf), lambda a, i, j: (0, j + nj)),
                  pl.BlockSpec((tf, D), lambda a, i, j: (j, 0))],
        out_specs=row,
        out_shape=jax.ShapeDtypeStruct((G, T, D), F32),
        scratch_shapes=[pltpu.VMEM((gb * tm, D), BF16), pltpu.VMEM((gb * tm, D), F32)],
        compiler_params=_cparams(("arbitrary", "arbitrary", "arbitrary")),
        name="ffn",
    )(x, gpre, sc, sh, gt, gpost, w_in, w_in, w_out)


def _pair_state(s):
    B, H, n, _ = s.shape
    return s.reshape(B, H // 2, 2, n, n).transpose(0, 1, 3, 2, 4).reshape(B, (H // 2) * n, 2 * n)


def _unpair_state(s, H):
    B = s.shape[0]
    n = A_HEAD_DIM
    return s.reshape(B, H // 2, n, 2, n).transpose(0, 1, 3, 2, 4).reshape(B, H, n, n)


def _trunk(x, mod, shift0, wkv0, moba_fn, W, gb, tm):
    G, T, D = x.shape
    aw = W["w0"].shape[-1]
    bw = D - aw
    ac = W["mu"].shape[-1]
    m2 = W["m2"]
    md = [mod[:, c:c + 1, :] for c in range(6)]
    sh_m, sc_m, gt_m, sh_f, sc_f, gt_f = md

    h = _norm_mod_rows(x, W["g_pre_mix"], sc_m, sh_m, gb, tm)
    tmm = 2 * tm if T % (2 * tm) == 0 else tm
    proj = lambda w, tn: _matmul(h, w, gb, tmm, tn)
    pa = proj(W["w_in_a"], ac // 2)
    q = proj(W["w_in_q"], bw)
    k = proj(W["w_in_k"], bw)
    v = proj(W["w_in_v"], bw)
    pg = proj(W["w_in_g"], 1024)

    tp = min(T, 256)
    r_, w_, k_, v_, a_, b_, c_, g_, bon, last = _rwkv_prep(
        pa, shift0[:, None, :], W["mu"], W["w0"], W["a0"], W["k_k"], W["k_a"], W["r_k"], W["wwa"], W["g2"], m2, tp)
    C = min(T, A_HEAD_DIM)
    npair = aw // LANES
    rs = lambda z: z.reshape(G, T, npair, LANES)
    yt, s_fin = _wkv_scan(rs(r_), rs(w_), rs(k_), rs(a_), rs(b_), rs(c_), v_, _pair_state(wkv0), C)
    o_a = _rwkv_post(yt, g_, bon, W["lnx_w"], W["lnx_b"], m2)

    o_b = moba_fn(q, k, v)

    x1 = _mixout(x, o_a, o_b, pg, gt_m, W["g_post_mix"], W["w_out_a"], W["w_out_b"], gb, min(tm, 256))
    y = _ffn(x1, W["g_pre_ffn"], sc_f, sh_f, gt_f, W["g_post_ffn"], W["w_ffn_in"], W["w_ffn_out"], gb, tm, 512)
    return y, last[:, 0, :], _unpair_state(s_fin, aw // A_HEAD_DIM), k, v


def kernel(x_prompt, x_sample, cache_k, cache_v, page_table, state_wkv, state_shift, c_prompt, c_sample, w_ada, b_ada, g_pre_mix, g_post_mix, g_pre_ffn, g_post_ffn, w_in, mu_shift, w0, w2, a0, a2, g2, k_k, k_a, r_k, lnx_w, lnx_b, w_out, w_ffn_in, w_ffn_out):
    depth = w_in.shape[0]
    assert depth == 1
    l = 0
    Bp, Tp, D = x_prompt.shape
    Bs, Ts, _ = x_sample.shape
    aw = w0.shape[-1]
    bw = D - aw
    ac = mu_shift.shape[-1]
    nh_a = aw // A_HEAD_DIM
    nh_b = bw // B_HEAD_DIM
    assert Bp == 1

    nc = Bp + Bs
    ncp = -(-nc // SUBLANES) * SUBLANES
    c_all = jnp.concatenate([c_prompt, c_sample, jnp.zeros((ncp - nc, D), F32)], axis=0)
    mod = _ada(c_all, w_ada[l], b_ada[l]).reshape(ncp, 6, D)
    mod_p, mod_s = mod[:Bp], mod[Bp:nc]

    wi = w_in[l]
    zeros_wa = jnp.zeros((DECAY_LORA, aw), F32)
    wwa = jnp.concatenate([jnp.concatenate([w2[l], zeros_wa], axis=1),
                           jnp.concatenate([zeros_wa, a2[l]], axis=1)], axis=0).astype(BF16)
    W = dict(
        m2=_block_ones(),
        g_pre_mix=g_pre_mix[l].reshape(1, 1, D), g_post_mix=g_post_mix[l].reshape(1, 1, D),
        g_pre_ffn=g_pre_ffn[l].reshape(1, 1, D), g_post_ffn=g_post_ffn[l].reshape(1, 1, D),
        w_in_a=wi[:, :ac].astype(BF16),
        w_in_q=wi[:, ac:ac + bw].astype(BF16),
        w_in_k=wi[:, ac + bw:ac + 2 * bw].astype(BF16),
        w_in_v=wi[:, ac + 2 * bw:ac + 3 * bw].astype(BF16),
        w_in_g=wi[:, ac + 3 * bw:].astype(BF16),
        mu=mu_shift[l].reshape(1, ac), w0=w0[l].reshape(1, aw), a0=a0[l].reshape(1, aw),
        k_k=k_k[l].reshape(1, aw), k_a=k_a[l].reshape(1, aw), r_k=r_k[l].reshape(1, aw),
        wwa=wwa, g2=g2[l].astype(BF16),
        lnx_w=lnx_w[l].reshape(1, aw), lnx_b=lnx_b[l].reshape(1, aw),
        w_out_a=w_out[l][:aw].astype(BF16), w_out_b=w_out[l][aw:].astype(BF16),
        w_ffn_in=w_ffn_in[l].astype(BF16), w_ffn_out=w_ffn_out[l].astype(BF16),
    )

    def moba_prompt_fn(q, k, v):
        kb, vt, km = _moba_prep(k[0], v[0])
        return _moba_prompt(q[0], kb, vt, km[:, 0, :])[None]

    def moba_sample_fn(q, k, v):
        o = _moba_sample(q, k, v, cache_k, cache_v, page_table)
        return o.reshape(Bs, nh_b, Ts, B_HEAD_DIM).transpose(0, 2, 1, 3).reshape(Bs, Ts, bw)

    zero_shift = jnp.zeros((Bp, ac), F32)
    zero_wkv = jnp.zeros((Bp, nh_a, A_HEAD_DIM, A_HEAD_DIM), F32)
    yp, shp, wkvp, kp, vp = _trunk(x_prompt, mod_p, zero_shift, zero_wkv, moba_prompt_fn, W, 1, 512)
    ys, shs, wkvs, ksm, vsm = _trunk(x_sample, mod_s, state_shift[l], state_wkv[l], moba_sample_fn, W, Bs, Ts)

    hd = lambda z, B, T: z.reshape(1, B, T, nh_b, B_HEAD_DIM)
    return (yp, ys, hd(kp, Bp, Tp), hd(vp, Bp, Tp), wkvp[None], shp[None],
            hd(ksm, Bs, Ts), hd(vsm, Bs, Ts), wkvs[None], shs[None])
```

```python
import functools

import numpy as np
import jax
import jax.numpy as jnp
from jax import lax
from jax.experimental import pallas as pl
from jax.experimental.pallas import tpu as pltpu

F32 = jnp.float32
BF16 = jnp.bfloat16

A_HEAD_DIM = 64
DECAY_LORA = 64
AAA_LORA = 64
GATE_LORA = 128
B_HEAD_DIM = 128
MOBA_BLOCK = 256
MOBA_TOPK = 3
PAGE_SIZE = 128
LNX_EPS = 64e-5
RMS_EPS = 1e-6
NEG = -1e30

LANES = 128
SUBLANES = 8
VMEM_LIMIT = 56 * 1024 * 1024
VMEM_LIMIT_ATTN = 60 * 1024 * 1024


def _cparams(sem):
    return pltpu.CompilerParams(dimension_semantics=sem, vmem_limit_bytes=VMEM_LIMIT)


def _split_bf16(x):
    hi = x.astype(BF16)
    lo = (x - hi.astype(F32)).astype(BF16)
    return hi, lo


def _seg_sum128(x, m2):
    hi, lo = _split_bf16(x)
    return jnp.dot(jnp.concatenate([hi, lo], axis=1), m2, preferred_element_type=F32)


def _seg_sum(x, m2):
    n = x.shape[1] // LANES
    return jnp.concatenate([_seg_sum128(x[:, c * LANES:(c + 1) * LANES], m2) for c in range(n)], axis=1)


def _block_ones_np():
    i = np.arange(LANES)
    return (i[:, None] // A_HEAD_DIM == i[None, :] // A_HEAD_DIM).astype(np.float32)


def _block_ones():
    m = _block_ones_np()
    return jnp.asarray(np.concatenate([m, m], axis=0), dtype=BF16)


def _ada_kernel(c_ref, w_ref, b_ref, o_ref):
    c = c_ref[...]
    s = (c * jax.nn.sigmoid(c)).astype(BF16)
    o_ref[...] = jnp.dot(s, w_ref[...].astype(BF16), preferred_element_type=F32) + b_ref[...]


def _ada(c, w_ada, b_ada):
    m, d = c.shape
    n = w_ada.shape[1]
    tn = 1024
    return pl.pallas_call(
        _ada_kernel,
        grid=(n // tn,),
        in_specs=[pl.BlockSpec((m, d), lambda j: (0, 0)),
                  pl.BlockSpec((d, tn), lambda j: (0, j)),
                  pl.BlockSpec((1, tn), lambda j: (0, j))],
        out_specs=pl.BlockSpec((m, tn), lambda j: (0, j)),
        out_shape=jax.ShapeDtypeStruct((m, n), F32),
        compiler_params=_cparams(("arbitrary",)),
        name="ada_mod",
    )(c, w_ada, b_ada.reshape(1, n))


def _norm_mod(x, g, sc, sh):
    ms = jnp.mean(x * x, axis=-1, keepdims=True)
    y = x * lax.rsqrt(ms + RMS_EPS) * g
    return y * (1.0 + sc) + sh


def _norm_mod_kernel(x_ref, g_ref, sc_ref, sh_ref, h_ref):
    h_ref[...] = _norm_mod(x_ref[...], g_ref[...], sc_ref[...], sh_ref[...]).astype(h_ref.dtype)


def _norm_mod_rows(x, g, sc, sh, gb, tm):
    G, T, D = x.shape
    row = pl.BlockSpec((gb, tm, D), lambda a, i: (a, i, 0))
    mod = pl.BlockSpec((gb, 1, D), lambda a, i: (a, 0, 0))
    return pl.pallas_call(
        _norm_mod_kernel,
        grid=(G // gb, T // tm),
        in_specs=[row, pl.BlockSpec((1, 1, D), lambda a, i: (0, 0, 0)), mod, mod],
        out_specs=row,
        out_shape=jax.ShapeDtypeStruct((G, T, D), BF16),
        compiler_params=_cparams(("arbitrary", "arbitrary")),
        name="norm_mod",
    )(x, g, sc, sh)


def _matmul_kernel(h_ref, w_ref, o_ref):
    gb, tm, d = h_ref.shape
    o = jnp.dot(h_ref[...].reshape(gb * tm, d), w_ref[...], preferred_element_type=F32)
    o_ref[...] = o.reshape(o_ref.shape)


def _matmul(h, w, gb, tm, tn):
    G, T, D = h.shape
    N = w.shape[1]
    return pl.pallas_call(
        _matmul_kernel,
        grid=(N // tn, G // gb, T // tm),
        in_specs=[pl.BlockSpec((gb, tm, D), lambda j, a, i: (a, i, 0)),
                  pl.BlockSpec((D, tn), lambda j, a, i: (0, j))],
        out_specs=pl.BlockSpec((gb, tm, tn), lambda j, a, i: (a, i, j)),
        out_shape=jax.ShapeDtypeStruct((G, T, N), F32),
        compiler_params=_cparams(("arbitrary", "arbitrary", "arbitrary")),
        name="proj_matmul",
    )(h, w)


def _softplus(z):
    return jnp.maximum(z, 0.0) + jnp.log(1.0 + jnp.exp(-jnp.abs(z)))


def _prep_kernel(pa_ref, sh0_ref, mu_ref, w0_ref, a0_ref, kk_ref, ka_ref, rk_ref, wwa_ref, g2_ref, m2_ref,
                 r_o, w_o, k_o, v_o, a_o, b_o, c_o, g_o, bon_o, last_o, carry_ref):
    tm = pa_ref.shape[1]
    aw = r_o.shape[2]

    @pl.when(pl.program_id(1) == 0)
    def _():
        carry_ref[...] = sh0_ref[0]

    pa = pa_ref[0]
    rolled = pltpu.roll(pa, 1, axis=0)
    row = lax.broadcasted_iota(jnp.int32, (tm, 1), 0)
    prev = jnp.where(row == 0, carry_ref[...], rolled)
    last = pa[tm - 1:tm, :]
    carry_ref[...] = last
    last_o[0] = last

    ps = pa + (prev - pa) * mu_ref[...]
    r = ps[:, 0:aw]
    k = ps[:, aw:2 * aw]
    v = ps[:, 2 * aw:3 * aw]
    o3 = 3 * aw
    x_wa = ps[:, o3:o3 + DECAY_LORA + AAA_LORA]
    gl = ps[:, o3 + DECAY_LORA + AAA_LORA:]
    lane = lax.broadcasted_iota(jnp.int32, x_wa.shape, 1)
    z = jnp.where(lane < DECAY_LORA, jnp.tanh(x_wa), x_wa).astype(BF16)
    wa = jnp.dot(z, wwa_ref[...], preferred_element_type=F32)
    lw = wa[:, :aw]
    la = wa[:, aw:]
    g = jnp.dot(jax.nn.sigmoid(gl).astype(BF16), g2_ref[...], preferred_element_type=F32)

    w_log = -_softplus(-(w0_ref[...] + lw)) - 0.5
    decay = jnp.exp(-jnp.exp(w_log))
    a = jax.nn.sigmoid(a0_ref[...] + la)
    m2 = m2_ref[...]
    kk = k * kk_ref[...]
    kk = kk * lax.rsqrt(jnp.maximum(_seg_sum(kk * kk, m2), 1e-24))
    k_mod = k * (1.0 + (a - 1.0) * ka_ref[...])
    bonus = _seg_sum(r * k_mod * rk_ref[...], m2) * v

    a_vec = -kk
    b_vec = kk * a
    a_next = pltpu.roll(a_vec, tm - 1, axis=0)
    r_o[0] = r
    w_o[0] = decay
    k_o[0] = k_mod
    v_o[0] = v
    a_o[0] = a_vec
    b_o[0] = b_vec
    c_o[0] = _seg_sum(b_vec * a_next, m2)
    g_o[0] = g
    bon_o[0] = bonus


def _rwkv_prep(pa, shift0, mu, w0, a0, k_k, k_a, r_k, wwa, g2, m2, tm):
    G, T, AC = pa.shape
    aw = w0.shape[-1]
    vec = lambda n: pl.BlockSpec((1, n), lambda a, i: (0, 0))
    full = lambda s: pl.BlockSpec(s, lambda a, i: (0, 0))
    big = pl.BlockSpec((1, tm, aw), lambda a, i: (a, i, 0))
    assert tm % 2 == 0
    outs = [jax.ShapeDtypeStruct((G, T, aw), F32)] * 9 + [jax.ShapeDtypeStruct((G, 1, AC), F32)]
    return pl.pallas_call(
        _prep_kernel,
        grid=(G, T // tm),
        in_specs=[pl.BlockSpec((1, tm, AC), lambda a, i: (a, i, 0)),
                  pl.BlockSpec((1, 1, AC), lambda a, i: (a, 0, 0)),
                  vec(AC), vec(aw), vec(aw), vec(aw), vec(aw), vec(aw),
                  full(wwa.shape), full(g2.shape), full(m2.shape)],
        out_specs=[big] * 9 + [pl.BlockSpec((1, 1, AC), lambda a, i: (a, 0, 0))],
        out_shape=outs,
        scratch_shapes=[pltpu.VMEM((1, AC), F32)],
        compiler_params=_cparams(("arbitrary", "arbitrary")),
        name="rwkv_prep",
    )(pa, shift0, mu, w0, a0, k_k, k_a, r_k, wwa, g2, m2)


def _half_lane_unzip(xt):
    lo_half = lax.broadcasted_iota(jnp.int32, (A_HEAD_DIM, LANES), 1) < A_HEAD_DIM
    top = xt[0:A_HEAD_DIM]
    bot = xt[A_HEAD_DIM:2 * A_HEAD_DIM]
    return (jnp.where(lo_half, top, pltpu.roll(bot, A_HEAD_DIM, axis=1)),
            jnp.where(lo_half, pltpu.roll(top, A_HEAD_DIM, axis=1), bot))


def _scan_kernel(r_ref, w_ref, k_ref, a_ref, b_ref, c_ref, v2_ref, s0_ref, my_ref, y_ref, sf_ref, s_ref,
                 vt_ref, q_ref, vb_ref):
    C = r_ref.shape[1]
    npair = r_ref.shape[2]
    rows = npair * A_HEAD_DIM

    @pl.when(pl.program_id(1) == 0)
    def _():
        s_ref[...] = s0_ref[0]

    v2 = v2_ref[0]
    if C < A_HEAD_DIM:
        v2 = jnp.concatenate([v2, jnp.zeros((A_HEAD_DIM - C, v2.shape[1]), F32)], axis=0)
    for pp in range(npair // 2):
        x = jnp.concatenate([v2[:, (2 * pp) * LANES:(2 * pp + 1) * LANES],
                             v2[:, (2 * pp + 1) * LANES:(2 * pp + 2) * LANES]], axis=0)
        for j, o in enumerate(_half_lane_unzip(x.T)):
            p = 2 * pp + j
            vt_ref[p * A_HEAD_DIM:(p + 1) * A_HEAD_DIM, :] = o

    half0 = (lax.broadcasted_iota(jnp.int32, (rows, LANES), 1) // A_HEAD_DIM) * A_HEAD_DIM

    def value_columns(j):
        vt = vt_ref[...]
        return jnp.concatenate(
            [jnp.take_along_axis(vt, half0 + 2 * j, axis=1, mode="promise_in_bounds"),
             jnp.take_along_axis(vt, half0 + (2 * j + 1), axis=1, mode="promise_in_bounds")], axis=1)

    y_ref[...] = jnp.zeros(y_ref.shape, F32)
    q_ref[...] = jnp.zeros(q_ref.shape, BF16)
    my = my_ref[...]
    tlane = lax.broadcasted_iota(jnp.int32, (rows, LANES), 1) % A_HEAD_DIM

    def bc(ref, t):
        return jnp.concatenate(
            [jnp.broadcast_to(ref[0, t, p:p + 1, :], (A_HEAD_DIM, LANES)) for p in range(npair)], axis=0)

    def put_y(j):
        yb = jnp.dot(q_ref[...], my, preferred_element_type=F32)
        y_ref[0, 0] = jnp.where(tlane == 2 * j, yb[:, :LANES],
                                jnp.where(tlane == 2 * j + 1, yb[:, LANES:], y_ref[0, 0]))

    def two_steps(j, carry):
        t0 = 2 * j
        t1 = t0 + 1
        vb = vb_ref[...]
        S0 = s_ref[...]
        X = S0 * bc(w_ref, t0) + vb[:, :LANES] * bc(k_ref, t0)
        lhs = jnp.concatenate([(S0 * bc(a_ref, t0)).astype(BF16), (X * bc(a_ref, t1)).astype(BF16)], axis=1)
        out = jnp.dot(lhs, my, preferred_element_type=F32)
        put_y(j - 1)
        vb_next = value_columns(jnp.minimum(j + 1, C // 2 - 1))
        sa0 = out[:, :LANES]
        sa1 = out[:, LANES:] + sa0 * bc(c_ref, t0)
        S1 = X + sa0 * bc(b_ref, t0)
        S2 = S1 * bc(w_ref, t1) + sa1 * bc(b_ref, t1) + vb[:, LANES:] * bc(k_ref, t1)
        s_ref[...] = S2
        q_ref[...] = jnp.concatenate([(S1 * bc(r_ref, t0)).astype(BF16), (S2 * bc(r_ref, t1)).astype(BF16)], axis=1)
        vb_ref[...] = vb_next
        return carry

    vb_ref[...] = value_columns(0)
    lax.fori_loop(0, C // 2, two_steps, 0, unroll=4)
    put_y(C // 2 - 1)
    sf_ref[0] = s_ref[...]


def _block_ones_pair():
    m = _block_ones_np()
    z = np.zeros_like(m)
    return jnp.asarray(np.block([[m, z], [z, m]]), dtype=BF16)


def _wkv_scan(r, w, k, a, b, c, v2, s0, C):
    G, T, npair, _ = r.shape
    rows = npair * A_HEAD_DIM
    assert C % 2 == 0
    my = _block_ones_pair()
    seq = pl.BlockSpec((1, C, npair, LANES), lambda g, c: (g, c, 0, 0))
    st = pl.BlockSpec((1, rows, LANES), lambda g, c: (g, 0, 0))
    return pl.pallas_call(
        _scan_kernel,
        grid=(G, T // C),
        in_specs=[seq] * 6 + [pl.BlockSpec((1, C, v2.shape[2]), lambda g, c: (g, c, 0)), st,
                              pl.BlockSpec(my.shape, lambda g, c: (0, 0))],
        out_specs=[pl.BlockSpec((1, 1, rows, LANES), lambda g, c: (g, c, 0, 0)), st],
        out_shape=[jax.ShapeDtypeStruct((G, T // C, rows, LANES), F32),
                   jax.ShapeDtypeStruct((G, rows, LANES), F32)],
        scratch_shapes=[pltpu.VMEM((rows, LANES), F32), pltpu.VMEM((rows, LANES), F32),
                        pltpu.VMEM((rows, 2 * LANES), BF16), pltpu.VMEM((rows, 2 * LANES), F32)],
        compiler_params=_cparams(("arbitrary", "arbitrary")),
        name="wkv_scan",
    )(r, w, k, a, b, c, v2, s0, my)


def _post_kernel(yt_ref, g_ref, bon_ref, lw_ref, lb_ref, m2_ref, o_ref):
    C = o_ref.shape[1]
    rows = yt_ref.shape[2]
    chunks = []
    for pp in range(rows // LANES):
        rt = yt_ref[0, 0, pp * LANES:(pp + 1) * LANES, :].T
        chunks.extend(o[:C] for o in _half_lane_unzip(rt))
    y = jnp.concatenate(chunks, axis=1)
    m2 = m2_ref[...]
    inv = 1.0 / A_HEAD_DIM
    mean = _seg_sum(y, m2) * inv
    yc = y - mean
    var = _seg_sum(yc * yc, m2) * inv
    yn = yc * lax.rsqrt(var + LNX_EPS) * lw_ref[...] + lb_ref[...]
    o_ref[0] = ((yn + bon_ref[0]) * g_ref[0]).astype(o_ref.dtype)


def _rwkv_post(yt, g, bon, lnx_w, lnx_b, m2):
    G, NC, rows, _ = yt.shape
    T, aw = g.shape[1], g.shape[2]
    C = T // NC
    big = pl.BlockSpec((1, C, aw), lambda a, c: (a, c, 0))
    vec = pl.BlockSpec((1, aw), lambda a, c: (0, 0))
    return pl.pallas_call(
        _post_kernel,
        grid=(G, NC),
        in_specs=[pl.BlockSpec((1, 1, rows, LANES), lambda a, c: (a, c, 0, 0)), big, big, vec, vec,
                  pl.BlockSpec(m2.shape, lambda a, c: (0, 0))],
        out_specs=big,
        out_shape=jax.ShapeDtypeStruct((G, T, aw), F32),
        compiler_params=_cparams(("arbitrary", "arbitrary")),
        name="rwkv_post",
    )(yt, g, bon, lnx_w, lnx_b, m2)


def _moba_prep_kernel(k_ref, v_ref, kb_ref, vt_ref, km_ref):
    k = k_ref[...]
    kb_ref[...] = k.astype(BF16)
    km_ref[0] = jnp.mean(k, axis=0, keepdims=True)
    vt_ref[...] = v_ref[...].T.astype(BF16)


def _moba_prep(k, v):
    T, W = k.shape
    nb = T // MOBA_BLOCK
    blk = pl.BlockSpec((MOBA_BLOCK, W), lambda n: (n, 0))
    return pl.pallas_call(
        _moba_prep_kernel,
        grid=(nb,),
        in_specs=[blk, blk],
        out_specs=[blk, pl.BlockSpec((W, MOBA_BLOCK), lambda n: (0, n)),
                   pl.BlockSpec((1, 1, W), lambda n: (n, 0, 0))],
        out_shape=[jax.ShapeDtypeStruct((T, W), BF16), jax.ShapeDtypeStruct((W, T), BF16),
                   jax.ShapeDtypeStruct((nb, 1, W), F32)],
        compiler_params=_cparams(("arbitrary",)),
        name="moba_prep",
    )(k, v)


def _select_topk(gate, valid, axis):
    idx = lax.broadcasted_iota(jnp.int32, gate.shape, axis)
    g = jnp.where(valid, gate, NEG)
    sel = jnp.zeros(gate.shape, jnp.bool_)
    for _ in range(MOBA_TOPK):
        m = jnp.max(g, axis=axis, keepdims=True)
        first = jnp.min(jnp.where(g == m, idx, jnp.int32(2 ** 30)), axis=axis, keepdims=True)
        pick = idx == first
        sel = jnp.logical_or(sel, pick)
        g = jnp.where(pick, -jnp.inf, g)
    return jnp.logical_and(sel, valid)


PROMPT_HEADS_PER_STEP = 2
PROMPT_GROUP = 8
LOG2E = 1.4426950408889634


def _moba_prompt_kernel(q_ref, k_ref, vt_ref, km_ref, o_ref, sel_ref, s_ref):
    i = pl.program_id(1)
    tq = q_ref.shape[0]
    nb = km_ref.shape[0]
    dh = B_HEAD_DIM
    nt = (((1,), (1,)), ((), ()))
    heads = range(PROMPT_HEADS_PER_STEP)
    blk = lax.broadcasted_iota(jnp.int32, (nb, tq), 0)
    key = lax.broadcasted_iota(jnp.int32, (MOBA_BLOCK, tq), 0)
    qi = lax.broadcasted_iota(jnp.int32, (MOBA_BLOCK, tq), 1)
    start = pl.multiple_of(i * MOBA_BLOCK, MOBA_BLOCK)

    qbs = []
    for u in heads:
        sl = slice(u * dh, (u + 1) * dh)
        q = q_ref[:, sl]
        qb = (q * (dh ** -0.5 * LOG2E)).astype(BF16)
        qbs.append(qb)
        gate_t = lax.dot_general(km_ref[:, sl], q, nt, precision=lax.Precision.HIGHEST,
                                 preferred_element_type=F32)
        sel_ref[u, 0:nb, :] = _select_topk(gate_t, blk < i, 0).astype(F32)
        sel_ref[u, nb:, :] = jnp.zeros((sel_ref.shape[1] - nb, tq), F32)

    def scores(n, u):
        st = pl.multiple_of(n * MOBA_BLOCK, MOBA_BLOCK)
        return lax.dot_general(k_ref[pl.ds(st, MOBA_BLOCK), u * dh:(u + 1) * dh], qbs[u], nt,
                               preferred_element_type=F32)

    def selected(n, u):
        return sel_ref[u, pl.ds(n, 1), :] > 0.5

    gk = PROMPT_GROUP * MOBA_BLOCK
    n_iter = (i + PROMPT_GROUP - 1) // PROMPT_GROUP

    def pass1(g, ms):
        st = pl.multiple_of(g * gk, gk)
        out = []
        for u in heads:
            m = ms[u]
            s_g = lax.dot_general(k_ref[pl.ds(st, gk), u * dh:(u + 1) * dh], qbs[u], nt,
                                  preferred_element_type=F32)
            s_ref[u, pl.ds(st, gk), :] = s_g
            for c in range(PROMPT_GROUP):
                mb = jnp.max(s_g[c * MOBA_BLOCK:(c + 1) * MOBA_BLOCK], axis=0, keepdims=True)
                m = jnp.maximum(m, jnp.where(selected(g * PROMPT_GROUP + c, u), mb, NEG))
            out.append(m)
        return tuple(out)

    ms = lax.fori_loop(0, n_iter, pass1, tuple(jnp.full((1, tq), NEG, F32) for _ in heads))

    init = []
    for u in heads:
        sl = slice(u * dh, (u + 1) * dh)
        s_t = jnp.where(key <= qi, scores(i, u), NEG)
        m = jnp.maximum(ms[u], jnp.max(s_t, axis=0, keepdims=True))
        p = jnp.exp2(s_t - m)
        l = jnp.sum(p, axis=0, keepdims=True)
        acc = jnp.dot(vt_ref[sl, pl.ds(start, MOBA_BLOCK)], p.astype(BF16), preferred_element_type=F32)
        init.append((m, l, acc))
    ms = tuple(m for m, _, _ in init)

    def pass2(g, carry):
        st = pl.multiple_of(g * gk, gk)
        out = []
        for u in heads:
            sl = slice(u * dh, (u + 1) * dh)
            l, acc = carry[u]
            ps = []
            for c in range(PROMPT_GROUP):
                n = g * PROMPT_GROUP + c
                s_t = s_ref[u, pl.ds(pl.multiple_of(n * MOBA_BLOCK, MOBA_BLOCK), MOBA_BLOCK), :]
                p = jnp.exp2(s_t - jnp.where(selected(n, u), ms[u], -NEG))
                l = l + jnp.sum(p, axis=0, keepdims=True)
                ps.append(p.astype(BF16))
            acc = acc + jnp.dot(vt_ref[sl, pl.ds(st, gk)], jnp.concatenate(ps, axis=0),
                                preferred_element_type=F32)
            out.append((l, acc))
        return tuple(out)

    fin = lax.fori_loop(0, n_iter, pass2, tuple((l, acc) for _, l, acc in init))
    for u in heads:
        l, acc = fin[u]
        o_ref[:, u * dh:(u + 1) * dh] = (acc / l).T


def _moba_prompt(q, kb, vt, kmean):
    T, W = q.shape
    hw = PROMPT_HEADS_PER_STEP * B_HEAD_DIM
    nb = T // MOBA_BLOCK
    tp = T if nb % PROMPT_GROUP == 0 else T + (PROMPT_GROUP - 1) * MOBA_BLOCK
    kb = jnp.pad(kb, ((0, tp - T), (0, 0)))
    vt = jnp.pad(vt, ((0, 0), (0, tp - T)))
    once = pl.Buffered(1)
    return pl.pallas_call(
        _moba_prompt_kernel,
        grid=(W // hw, nb),
        in_specs=[pl.BlockSpec((MOBA_BLOCK, hw), lambda h, i: (i, h)),
                  pl.BlockSpec((tp, hw), lambda h, i: (0, h), pipeline_mode=once),
                  pl.BlockSpec((hw, tp), lambda h, i: (h, 0), pipeline_mode=once),
                  pl.BlockSpec((nb, hw), lambda h, i: (0, h))],
        out_specs=pl.BlockSpec((MOBA_BLOCK, hw), lambda h, i: (i, h)),
        out_shape=jax.ShapeDtypeStruct((T, W), F32),
        scratch_shapes=[pltpu.VMEM((PROMPT_HEADS_PER_STEP, nb + SUBLANES, MOBA_BLOCK), F32),
                        pltpu.VMEM((PROMPT_HEADS_PER_STEP, tp, MOBA_BLOCK), F32)],
        compiler_params=pltpu.CompilerParams(dimension_semantics=("arbitrary", "arbitrary"),
                                             vmem_limit_bytes=VMEM_LIMIT_ATTN),
        name="moba_prompt",
    )(q, kb, vt, kmean)


SAMPLE_PAGES_PER_STEP = 16


def _moba_sample_kernel(pt_ref, q_ref, kn_ref, vn_ref, *rest):
    P = SAMPLE_PAGES_PER_STEP
    kp = rest[:P]
    vp = rest[P:2 * P]
    o_ref = rest[2 * P]
    km_ref, m_ref, l_ref, oacc_ref = rest[2 * P + 1:]
    s = pl.program_id(1)
    ns = pl.num_programs(1)
    tq, W = q_ref.shape[1], q_ref.shape[2]
    dh = B_HEAD_DIM
    nh = W // dh
    hq = nh * tq
    nbp = km_ref.shape[1]
    ppb = MOBA_BLOCK // PAGE_SIZE
    bps = P // ppb
    nt = (((1,), (1,)), ((), ()))

    def by_head(x):
        return jnp.concatenate([x[:, h * dh:(h + 1) * dh] for h in range(nh)], axis=0)

    qall = by_head(q_ref[0])
    qall_b = (qall * (dh ** -0.5)).astype(BF16)

    def partial(k_b, v_b, mask):
        sc = jnp.where(mask, lax.dot_general(qall_b, k_b, nt, preferred_element_type=F32), NEG)
        m = jnp.max(sc, axis=1, keepdims=True)
        p = jnp.exp(sc - m)
        l = jnp.sum(p, axis=1, keepdims=True)
        o = jnp.dot(p.astype(BF16), v_b, preferred_element_type=F32)
        return jnp.broadcast_to(m, (hq, dh)), jnp.broadcast_to(l, (hq, dh)), o

    nflat = MOBA_BLOCK * nh
    rowh = lax.broadcasted_iota(jnp.int32, (hq, nflat), 0) // tq
    colh = lax.broadcasted_iota(jnp.int32, (hq, nflat), 1) % nh
    same_head = rowh == colh
    for j in range(bps):
        kpages = [kp[j * ppb + u][0, 0] for u in range(ppb)]
        vpages = [vp[j * ppb + u][0, 0] for u in range(ppb)]
        n = s * bps + j
        kmean = sum(jnp.sum(pg, axis=0) for pg in kpages) * (1.0 / MOBA_BLOCK)
        for h in range(nh):
            km_ref[h, pl.ds(n, 1), :] = kmean[h:h + 1, :]
        kflat = jnp.concatenate([pg.reshape(PAGE_SIZE * nh, dh) for pg in kpages], axis=0).astype(BF16)
        vflat = jnp.concatenate([pg.reshape(PAGE_SIZE * nh, dh) for pg in vpages], axis=0).astype(BF16)
        m, l, o = partial(kflat, vflat, same_head)
        m_ref[n] = m
        l_ref[n] = l
        oacc_ref[n] = o

    @pl.when(s == ns - 1)
    def _():
        pad = jnp.zeros((LANES - hq, dh), F32)
        kn = jnp.concatenate([by_head(kn_ref[0]), pad], axis=0).astype(BF16)
        vn = jnp.concatenate([by_head(vn_ref[0]), pad], axis=0).astype(BF16)
        col = lax.broadcasted_iota(jnp.int32, (hq, LANES), 1)
        row = lax.broadcasted_iota(jnp.int32, (hq, LANES), 0)
        own_mask = jnp.logical_and(col // tq == row // tq, col % tq <= row % tq)
        m_own, l_own, o_own = partial(kn, vn, own_mask)

        gate = jnp.concatenate(
            [lax.dot_general(qall[h * tq:(h + 1) * tq], km_ref[h], nt, precision=lax.Precision.HIGHEST,
                             preferred_element_type=F32) for h in range(nh)], axis=0)
        sel = _select_topk(gate, jnp.ones(gate.shape, jnp.bool_), 1).astype(F32)
        mx = m_own
        for n in range(nbp):
            sn = sel[:, n:n + 1] > 0.5
            mx = jnp.where(sn, jnp.maximum(mx, m_ref[n]), mx)
        w_own = jnp.exp(m_own - mx)
        lsum = w_own * l_own
        osum = w_own * o_own
        for n in range(nbp):
            sn = sel[:, n:n + 1] > 0.5
            wn = jnp.where(sn, jnp.exp(m_ref[n] - mx), 0.0)
            lsum = lsum + wn * l_ref[n]
            osum = osum + wn * oacc_ref[n]
        o_ref[0] = osum / lsum


def _moba_sample(q, kn, vn, cache_k, cache_v, page_table):
    B, tq, W = q.shape
    nh = W // B_HEAD_DIM
    hq = nh * tq
    assert hq <= LANES
    n_pages = page_table.shape[1]
    P = SAMPLE_PAGES_PER_STEP
    nbp = n_pages * PAGE_SIZE // MOBA_BLOCK
    small = pl.BlockSpec((1, tq, W), lambda b, s, pt: (b, 0, 0))

    def page_spec(j):
        return pl.BlockSpec((1, 1, PAGE_SIZE, nh, B_HEAD_DIM), lambda b, s, pt: (0, pt[b, s * P + j], 0, 0, 0))

    grid_spec = pltpu.PrefetchScalarGridSpec(
        num_scalar_prefetch=1,
        grid=(B, n_pages // P),
        in_specs=[small, small, small] + [page_spec(j) for j in range(P)] * 2,
        out_specs=pl.BlockSpec((1, hq, B_HEAD_DIM), lambda b, s, pt: (b, 0, 0)),
        scratch_shapes=[pltpu.VMEM((nh, nbp, B_HEAD_DIM), F32),
                        pltpu.VMEM((nbp, hq, B_HEAD_DIM), F32), pltpu.VMEM((nbp, hq, B_HEAD_DIM), F32),
                        pltpu.VMEM((nbp, hq, B_HEAD_DIM), F32)],
    )
    return pl.pallas_call(
        _moba_sample_kernel,
        grid_spec=grid_spec,
        out_shape=jax.ShapeDtypeStruct((B, hq, B_HEAD_DIM), F32),
        compiler_params=_cparams(("arbitrary", "arbitrary")),
        name="moba_sample",
    )(page_table, q, kn, vn, *([cache_k] * P), *([cache_v] * P))


def _mixout_kernel(x_ref, oa_ref, ob_ref, pg_ref, gt_ref, gp_ref, wa_ref, wb_ref, o_ref):
    gb, tm, d = x_ref.shape
    aw = oa_ref.shape[2]
    pg = pg_ref[...]
    ma = (oa_ref[...] * jax.nn.sigmoid(pg[..., :aw])).reshape(gb * tm, aw).astype(BF16)
    mb = (ob_ref[...] * jax.nn.sigmoid(pg[..., aw:])).reshape(gb * tm, pg.shape[2] - aw).astype(BF16)
    f = jnp.dot(ma, wa_ref[...], preferred_element_type=F32) + jnp.dot(mb, wb_ref[...], preferred_element_type=F32)
    y = f * lax.rsqrt(jnp.mean(f * f, axis=-1, keepdims=True) + RMS_EPS)
    y = y.reshape(gb, tm, d) * gp_ref[...]
    o_ref[...] = x_ref[...] + gt_ref[...] * y


def _mixout(x, oa, ob, pg, gt, gpost, wa, wb, gb, tm):
    G, T, D = x.shape
    aw, bw = oa.shape[2], ob.shape[2]
    row = lambda n: pl.BlockSpec((gb, tm, n), lambda a, i: (a, i, 0))
    return pl.pallas_call(
        _mixout_kernel,
        grid=(G // gb, T // tm),
        in_specs=[row(D), row(aw), row(bw), row(aw + bw),
                  pl.BlockSpec((gb, 1, D), lambda a, i: (a, 0, 0)),
                  pl.BlockSpec((1, 1, D), lambda a, i: (0, 0, 0)),
                  pl.BlockSpec(wa.shape, lambda a, i: (0, 0)),
                  pl.BlockSpec(wb.shape, lambda a, i: (0, 0))],
        out_specs=row(D),
        out_shape=jax.ShapeDtypeStruct((G, T, D), F32),
        compiler_params=_cparams(("arbitrary", "arbitrary")),
        name="mix_out",
    )(x, oa, ob, pg, gt, gpost, wa, wb)


def _ffn_kernel(x_ref, gpre_ref, sc_ref, sh_ref, gt_ref, gpost_ref, wg_ref, wu_ref, wo_ref, o_ref, h_ref, acc_ref):
    gb, tm, d = x_ref.shape
    j = pl.program_id(2)

    @pl.when(j == 0)
    def _():
        h = _norm_mod(x_ref[...], gpre_ref[...], sc_ref[...], sh_ref[...])
        h_ref[...] = h.reshape(gb * tm, d).astype(BF16)
        acc_ref[...] = jnp.zeros(acc_ref.shape, F32)

    h = h_ref[...]
    g = jnp.dot(h, wg_ref[...], preferred_element_type=F32)
    u = jnp.dot(h, wu_ref[...], preferred_element_type=F32)
    act = (g * jax.nn.sigmoid(g) * u).astype(BF16)
    acc_ref[...] += jnp.dot(act, wo_ref[...], preferred_element_type=F32)

    @pl.when(j == pl.num_programs(2) - 1)
    def _():
        f = acc_ref[...]
        y = f * lax.rsqrt(jnp.mean(f * f, axis=-1, keepdims=True) + RMS_EPS)
        y = y.reshape(gb, tm, d) * gpost_ref[...]
        o_ref[...] = x_ref[...] + gt_ref[...] * y


def _ffn(x, gpre, sc, sh, gt, gpost, w_in, w_out, gb, tm, tf):
    G, T, D = x.shape
    dff = w_out.shape[0]
    nj = dff // tf
    row = pl.BlockSpec((gb, tm, D), lambda a, i, j: (a, i, 0))
    mod = pl.BlockSpec((gb, 1, D), lambda a, i, j: (a, 0, 0))
    gain = pl.BlockSpec((1, 1, D), lambda a, i, j: (0, 0, 0))
    return pl.pallas_call(
        _ffn_kernel,
        grid=(G // gb, T // tm, nj),
        in_specs=[row, gain, mod, mod, mod, gain,
                  pl.BlockSpec((D, tf), lambda a, i, j: (0, j)),
                  pl.BlockSpec((D, tf), lambda a, i, j: (0, j + nj)),
                  pl.BlockSpec((tf, D), lambda a, i, j: (j, 0))],
        out_specs=row,
        out_shape=jax.ShapeDtypeStruct((G, T, D), F32),
        scratch_shapes=[pltpu.VMEM((gb * tm, D), BF16), pltpu.VMEM((gb * tm, D), F32)],
        compiler_params=_cparams(("arbitrary", "arbitrary", "arbitrary")),
        name="ffn",
    )(x, gpre, sc, sh, gt, gpost, w_in, w_in, w_out)


def _pair_state(s):
    B, H, n, _ = s.shape
    return s.reshape(B, H // 2, 2, n, n).transpose(0, 1, 3, 2, 4).reshape(B, (H // 2) * n, 2 * n)


def _unpair_state(s, H):
    B = s.shape[0]
    n = A_HEAD_DIM
    return s.reshape(B, H // 2, n, 2, n).transpose(0, 1, 3, 2, 4).reshape(B, H, n, n)


def _trunk(x, mod, shift0, wkv0, moba_fn, W, gb, tm):
    G, T, D = x.shape
    aw = W["w0"].shape[-1]
    bw = D - aw
    ac = W["mu"].shape[-1]
    m2 = W["m2"]
    md = [mod[:, c:c + 1, :] for c in range(6)]
    sh_m, sc_m, gt_m, sh_f, sc_f, gt_f = md

    h = _norm_mod_rows(x, W["g_pre_mix"], sc_m, sh_m, gb, tm)
    tmm = 2 * tm if T % (2 * tm) == 0 else tm
    proj = lambda w, tn: _matmul(h, w, gb, tmm, tn)
    pa = proj(W["w_in_a"], ac // 2)
    q = proj(W["w_in_q"], bw)
    k = proj(W["w_in_k"], bw)
    v = proj(W["w_in_v"], bw)
    pg = proj(W["w_in_g"], 1024)

    tp = min(T, 256)
    r_, w_, k_, v_, a_, b_, c_, g_, bon, last = _rwkv_prep(
        pa, shift0[:, None, :], W["mu"], W["w0"], W["a0"], W["k_k"], W["k_a"], W["r_k"], W["wwa"], W["g2"], m2, tp)
    C = min(T, A_HEAD_DIM)
    npair = aw // LANES
    rs = lambda z: z.reshape(G, T, npair, LANES)
    yt, s_fin = _wkv_scan(rs(r_), rs(w_), rs(k_), rs(a_), rs(b_), rs(c_), v_, _pair_state(wkv0), C)
    o_a = _rwkv_post(yt, g_, bon, W["lnx_w"], W["lnx_b"], m2)

    o_b = moba_fn(q, k, v)

    x1 = _mixout(x, o_a, o_b, pg, gt_m, W["g_post_mix"], W["w_out_a"], W["w_out_b"], gb, min(tm, 256))
    y = _ffn(x1, W["g_pre_ffn"], sc_f, sh_f, gt_f, W["g_post_ffn"], W["w_ffn_in"], W["w_ffn_out"], gb, tm, 512)
    return y, last[:, 0, :], _unpair_state(s_fin, aw // A_HEAD_DIM), k, v


def kernel(x_prompt, x_sample, cache_k, cache_v, page_table, state_wkv, state_shift, c_prompt, c_sample, w_ada, b_ada, g_pre_mix, g_post_mix, g_pre_ffn, g_post_ffn, w_in, mu_shift, w0, w2, a0, a2, g2, k_k, k_a, r_k, lnx_w, lnx_b, w_out, w_ffn_in, w_ffn_out):
    depth = w_in.shape[0]
    assert depth == 1
    l = 0
    Bp, Tp, D = x_prompt.shape
    Bs, Ts, _ = x_sample.shape
    aw = w0.shape[-1]
    bw = D - aw
    ac = mu_shift.shape[-1]
    nh_a = aw // A_HEAD_DIM
    nh_b = bw // B_HEAD_DIM
    assert Bp == 1

    nc = Bp + Bs
    ncp = -(-nc // SUBLANES) * SUBLANES
    c_all = jnp.concatenate([c_prompt, c_sample, jnp.zeros((ncp - nc, D), F32)], axis=0)
    mod = _ada(c_all, w_ada[l], b_ada[l]).reshape(ncp, 6, D)
    mod_p, mod_s = mod[:Bp], mod[Bp:nc]

    wi = w_in[l]
    zeros_wa = jnp.zeros((DECAY_LORA, aw), F32)
    wwa = jnp.concatenate([jnp.concatenate([w2[l], zeros_wa], axis=1),
                           jnp.concatenate([zeros_wa, a2[l]], axis=1)], axis=0).astype(BF16)
    W = dict(
        m2=_block_ones(),
        g_pre_mix=g_pre_mix[l].reshape(1, 1, D), g_post_mix=g_post_mix[l].reshape(1, 1, D),
        g_pre_ffn=g_pre_ffn[l].reshape(1, 1, D), g_post_ffn=g_post_ffn[l].reshape(1, 1, D),
        w_in_a=wi[:, :ac].astype(BF16),
        w_in_q=wi[:, ac:ac + bw].astype(BF16),
        w_in_k=wi[:, ac + bw:ac + 2 * bw].astype(BF16),
        w_in_v=wi[:, ac + 2 * bw:ac + 3 * bw].astype(BF16),
        w_in_g=wi[:, ac + 3 * bw:].astype(BF16),
        mu=mu_shift[l].reshape(1, ac), w0=w0[l].reshape(1, aw), a0=a0[l].reshape(1, aw),
        k_k=k_k[l].reshape(1, aw), k_a=k_a[l].reshape(1, aw), r_k=r_k[l].reshape(1, aw),
        wwa=wwa, g2=g2[l].astype(BF16),
        lnx_w=lnx_w[l].reshape(1, aw), lnx_b=lnx_b[l].reshape(1, aw),
        w_out_a=w_out[l][:aw].astype(BF16), w_out_b=w_out[l][aw:].astype(BF16),
        w_ffn_in=w_ffn_in[l].astype(BF16), w_ffn_out=w_ffn_out[l].astype(BF16),
    )

    def moba_prompt_fn(q, k, v):
        kb, vt, km = _moba_prep(k[0], v[0])
        return _moba_prompt(q[0], kb, vt, km[:, 0, :])[None]

    def moba_sample_fn(q, k, v):
        o = _moba_sample(q, k, v, cache_k, cache_v, page_table)
        return o.reshape(Bs, nh_b, Ts, B_HEAD_DIM).transpose(0, 2, 1, 3).reshape(Bs, Ts, bw)

    zero_shift = jnp.zeros((Bp, ac), F32)
    zero_wkv = jnp.zeros((Bp, nh_a, A_HEAD_DIM, A_HEAD_DIM), F32)
    yp, shp, wkvp, kp, vp = _trunk(x_prompt, mod_p, zero_shift, zero_wkv, moba_prompt_fn, W, 1, 512)
    ys, shs, wkvs, ksm, vsm = _trunk(x_sample, mod_s, state_shift[l], state_wkv[l], moba_sample_fn, W, Bs, Ts)

    hd = lambda z, B, T: z.reshape(1, B, T, nh_b, B_HEAD_DIM)
    return (yp, ys, hd(kp, Bp, Tp), hd(vp, Bp, Tp), wkvp[None], shp[None],
            hd(ksm, Bs, Ts), hd(vsm, Bs, Ts), wkvs[None], shs[None])
```

```python
import functools

import numpy as np
import jax
import jax.numpy as jnp
from jax import lax
from jax.experimental import pallas as pl
from jax.experimental.pallas import tpu as pltpu

F32 = jnp.float32
BF16 = jnp.bfloat16

A_HEAD_DIM = 64
DECAY_LORA = 64
AAA_LORA = 64
GATE_LORA = 128
B_HEAD_DIM = 128
MOBA_BLOCK = 256
MOBA_TOPK = 3
PAGE_SIZE = 128
LNX_EPS = 64e-5
RMS_EPS = 1e-6
NEG = -1e30

LANES = 128
SUBLANES = 8
VMEM_LIMIT = 56 * 1024 * 1024
VMEM_LIMIT_ATTN = 60 * 1024 * 1024


def _cparams(sem):
    return pltpu.CompilerParams(dimension_semantics=sem, vmem_limit_bytes=VMEM_LIMIT)


def _split_bf16(x):
    hi = x.astype(BF16)
    lo = (x - hi.astype(F32)).astype(BF16)
    return hi, lo


def _seg_sum128(x, m2):
    hi, lo = _split_bf16(x)
    return jnp.dot(jnp.concatenate([hi, lo], axis=1), m2, preferred_element_type=F32)


def _seg_sum(x, m2):
    n = x.shape[1] // LANES
    return jnp.concatenate([_seg_sum128(x[:, c * LANES:(c + 1) * LANES], m2) for c in range(n)], axis=1)


def _block_ones_np():
    i = np.arange(LANES)
    return (i[:, None] // A_HEAD_DIM == i[None, :] // A_HEAD_DIM).astype(np.float32)


def _block_ones():
    m = _block_ones_np()
    return jnp.asarray(np.concatenate([m, m], axis=0), dtype=BF16)


def _ada_kernel(c_ref, w_ref, b_ref, o_ref):
    c = c_ref[...]
    s = (c * jax.nn.sigmoid(c)).astype(BF16)
    o_ref[...] = jnp.dot(s, w_ref[...].astype(BF16), preferred_element_type=F32) + b_ref[...]


def _ada(c, w_ada, b_ada):
    m, d = c.shape
    n = w_ada.shape[1]
    tn = 1024
    return pl.pallas_call(
        _ada_kernel,
        grid=(n // tn,),
        in_specs=[pl.BlockSpec((m, d), lambda j: (0, 0)),
                  pl.BlockSpec((d, tn), lambda j: (0, j)),
                  pl.BlockSpec((1, tn), lambda j: (0, j))],
        out_specs=pl.BlockSpec((m, tn), lambda j: (0, j)),
        out_shape=jax.ShapeDtypeStruct((m, n), F32),
        compiler_params=_cparams(("arbitrary",)),
        name="ada_mod",
    )(c, w_ada, b_ada.reshape(1, n))


def _norm_mod(x, g, sc, sh):
    ms = jnp.mean(x * x, axis=-1, keepdims=True)
    y = x * lax.rsqrt(ms + RMS_EPS) * g
    return y * (1.0 + sc) + sh


def _norm_mod_kernel(x_ref, g_ref, sc_ref, sh_ref, h_ref):
    h_ref[...] = _norm_mod(x_ref[...], g_ref[...], sc_ref[...], sh_ref[...]).astype(h_ref.dtype)


def _norm_mod_rows(x, g, sc, sh, gb, tm):
    G, T, D = x.shape
    row = pl.BlockSpec((gb, tm, D), lambda a, i: (a, i, 0))
    mod = pl.BlockSpec((gb, 1, D), lambda a, i: (a, 0, 0))
    return pl.pallas_call(
        _norm_mod_kernel,
        grid=(G // gb, T // tm),
        in_specs=[row, pl.BlockSpec((1, 1, D), lambda a, i: (0, 0, 0)), mod, mod],
        out_specs=row,
        out_shape=jax.ShapeDtypeStruct((G, T, D), BF16),
        compiler_params=_cparams(("arbitrary", "arbitrary")),
        name="norm_mod",
    )(x, g, sc, sh)


def _matmul_kernel(h_ref, w_ref, o_ref):
    gb, tm, d = h_ref.shape
    o = jnp.dot(h_ref[...].reshape(gb * tm, d), w_ref[...], preferred_element_type=F32)
    o_ref[...] = o.reshape(o_ref.shape)


def _matmul(h, w, gb, tm, tn):
    G, T, D = h.shape
    N = w.shape[1]
    return pl.pallas_call(
        _matmul_kernel,
        grid=(N // tn, G // gb, T // tm),
        in_specs=[pl.BlockSpec((gb, tm, D), lambda j, a, i: (a, i, 0)),
                  pl.BlockSpec((D, tn), lambda j, a, i: (0, j))],
        out_specs=pl.BlockSpec((gb, tm, tn), lambda j, a, i: (a, i, j)),
        out_shape=jax.ShapeDtypeStruct((G, T, N), F32),
        compiler_params=_cparams(("arbitrary", "arbitrary", "arbitrary")),
        name="proj_matmul",
    )(h, w)


def _softplus(z):
    return jnp.maximum(z, 0.0) + jnp.log(1.0 + jnp.exp(-jnp.abs(z)))


def _prep_kernel(pa_ref, sh0_ref, mu_ref, w0_ref, a0_ref, kk_ref, ka_ref, rk_ref, wwa_ref, g2_ref, m2_ref,
                 r_o, w_o, k_o, v_o, a_o, b_o, c_o, g_o, bon_o, last_o, carry_ref):
    tm = pa_ref.shape[1]
    aw = r_o.shape[2]

    @pl.when(pl.program_id(1) == 0)
    def _():
        carry_ref[...] = sh0_ref[0]

    pa = pa_ref[0]
    rolled = pltpu.roll(pa, 1, axis=0)
    row = lax.broadcasted_iota(jnp.int32, (tm, 1), 0)
    prev = jnp.where(row == 0, carry_ref[...], rolled)
    last = pa[tm - 1:tm, :]
    carry_ref[...] = last
    last_o[0] = last

    ps = pa + (prev - pa) * mu_ref[...]
    r = ps[:, 0:aw]
    k = ps[:, aw:2 * aw]
    v = ps[:, 2 * aw:3 * aw]
    o3 = 3 * aw
    x_wa = ps[:, o3:o3 + DECAY_LORA + AAA_LORA]
    gl = ps[:, o3 + DECAY_LORA + AAA_LORA:]
    lane = lax.broadcasted_iota(jnp.int32, x_wa.shape, 1)
    z = jnp.where(lane < DECAY_LORA, jnp.tanh(x_wa), x_wa).astype(BF16)
    wa = jnp.dot(z, wwa_ref[...], preferred_element_type=F32)
    lw = wa[:, :aw]
    la = wa[:, aw:]
    g = jnp.dot(jax.nn.sigmoid(gl).astype(BF16), g2_ref[...], preferred_element_type=F32)

    w_log = -_softplus(-(w0_ref[...] + lw)) - 0.5
    decay = jnp.exp(-jnp.exp(w_log))
    a = jax.nn.sigmoid(a0_ref[...] + la)
    m2 = m2_ref[...]
    kk = k * kk_ref[...]
    kk = kk * lax.rsqrt(jnp.maximum(_seg_sum(kk * kk, m2), 1e-24))
    k_mod = k * (1.0 + (a - 1.0) * ka_ref[...])
    bonus = _seg_sum(r * k_mod * rk_ref[...], m2) * v

    a_vec = -kk
    b_vec = kk * a
    a_next = pltpu.roll(a_vec, tm - 1, axis=0)
    r_o[0] = r
    w_o[0] = decay
    k_o[0] = k_mod
    v_o[0] = v
    a_o[0] = a_vec
    b_o[0] = b_vec
    c_o[0] = _seg_sum(b_vec * a_next, m2)
    g_o[0] = g
    bon_o[0] = bonus


def _rwkv_prep(pa, shift0, mu, w0, a0, k_k, k_a, r_k, wwa, g2, m2, tm):
    G, T, AC = pa.shape
    aw = w0.shape[-1]
    vec = lambda n: pl.BlockSpec((1, n), lambda a, i: (0, 0))
    full = lambda s: pl.BlockSpec(s, lambda a, i: (0, 0))
    big = pl.BlockSpec((1, tm, aw), lambda a, i: (a, i, 0))
    assert tm % 2 == 0
    outs = [jax.ShapeDtypeStruct((G, T, aw), F32)] * 9 + [jax.ShapeDtypeStruct((G, 1, AC), F32)]
    return pl.pallas_call(
        _prep_kernel,
        grid=(G, T // tm),
        in_specs=[pl.BlockSpec((1, tm, AC), lambda a, i: (a, i, 0)),
                  pl.BlockSpec((1, 1, AC), lambda a, i: (a, 0, 0)),
                  vec(AC), vec(aw), vec(aw), vec(aw), vec(aw), vec(aw),
                  full(wwa.shape), full(g2.shape), full(m2.shape)],
        out_specs=[big] * 9 + [pl.BlockSpec((1, 1, AC), lambda a, i: (a, 0, 0))],
        out_shape=outs,
        scratch_shapes=[pltpu.VMEM((1, AC), F32)],
        compiler_params=_cparams(("arbitrary", "arbitrary")),
        name="rwkv_prep",
    )(pa, shift0, mu, w0, a0, k_k, k_a, r_k, wwa, g2, m2)


def _half_lane_unzip(xt):
    lo_half = lax.broadcasted_iota(jnp.int32, (A_HEAD_DIM, LANES), 1) < A_HEAD_DIM
    top = xt[0:A_HEAD_DIM]
    bot = xt[A_HEAD_DIM:2 * A_HEAD_DIM]
    return (jnp.where(lo_half, top, pltpu.roll(bot, A_HEAD_DIM, axis=1)),
            jnp.where(lo_half, pltpu.roll(top, A_HEAD_DIM, axis=1), bot))


def _scan_kernel(r_ref, w_ref, k_ref, a_ref, b_ref, c_ref, v2_ref, s0_ref, my_ref, y_ref, sf_ref, s_ref,
                 vt_ref, q_ref, vb_ref):
    C = r_ref.shape[1]
    npair = r_ref.shape[2]
    rows = npair * A_HEAD_DIM

    @pl.when(pl.program_id(1) == 0)
    def _():
        s_ref[...] = s0_ref[0]

    v2 = v2_ref[0]
    if C < A_HEAD_DIM:
        v2 = jnp.concatenate([v2, jnp.zeros((A_HEAD_DIM - C, v2.shape[1]), F32)], axis=0)
    for pp in range(npair // 2):
        x = jnp.concatenate([v2[:, (2 * pp) * LANES:(2 * pp + 1) * LANES],
                             v2[:, (2 * pp + 1) * LANES:(2 * pp + 2) * LANES]], axis=0)
        for j, o in enumerate(_half_lane_unzip(x.T)):
            p = 2 * pp + j
            vt_ref[p * A_HEAD_DIM:(p + 1) * A_HEAD_DIM, :] = o

    half0 = (lax.broadcasted_iota(jnp.int32, (rows, LANES), 1) // A_HEAD_DIM) * A_HEAD_DIM

    def value_columns(j):
        vt = vt_ref[...]
        return jnp.concatenate(
            [jnp.take_along_axis(vt, half0 + 2 * j, axis=1, mode="promise_in_bounds"),
             jnp.take_along_axis(vt, half0 + (2 * j + 1), axis=1, mode="promise_in_bounds")], axis=1)

    y_ref[...] = jnp.zeros(y_ref.shape, F32)
    q_ref[...] = jnp.zeros(q_ref.shape, BF16)
    my = my_ref[...]
    tlane = lax.broadcasted_iota(jnp.int32, (rows, LANES), 1) % A_HEAD_DIM

    def bc(ref, t):
        return jnp.concatenate(
            [jnp.broadcast_to(ref[0, t, p:p + 1, :], (A_HEAD_DIM, LANES)) for p in range(npair)], axis=0)

    def put_y(j):
        yb = jnp.dot(q_ref[...], my, preferred_element_type=F32)
        y_ref[0, 0] = jnp.where(tlane == 2 * j, yb[:, :LANES],
                                jnp.where(tlane == 2 * j + 1, yb[:, LANES:], y_ref[0, 0]))

    def two_steps(j, carry):
        t0 = 2 * j
        t1 = t0 + 1
        vb = vb_ref[...]
        S0 = s_ref[...]
        X = S0 * bc(w_ref, t0) + vb[:, :LANES] * bc(k_ref, t0)
        lhs = jnp.concatenate([(S0 * bc(a_ref, t0)).astype(BF16), (X * bc(a_ref, t1)).astype(BF16)], axis=1)
        out = jnp.dot(lhs, my, preferred_element_type=F32)
        put_y(j - 1)
        vb_next = value_columns(jnp.minimum(j + 1, C // 2 - 1))
        sa0 = out[:, :LANES]
        sa1 = out[:, LANES:] + sa0 * bc(c_ref, t0)
        S1 = X + sa0 * bc(b_ref, t0)
        S2 = S1 * bc(w_ref, t1) + sa1 * bc(b_ref, t1) + vb[:, LANES:] * bc(k_ref, t1)
        s_ref[...] = S2
        q_ref[...] = jnp.concatenate([(S1 * bc(r_ref, t0)).astype(BF16), (S2 * bc(r_ref, t1)).astype(BF16)], axis=1)
        vb_ref[...] = vb_next
        return carry

    vb_ref[...] = value_columns(0)
    lax.fori_loop(0, C // 2, two_steps, 0, unroll=4)
    put_y(C // 2 - 1)
    sf_ref[0] = s_ref[...]


def _block_ones_pair():
    m = _block_ones_np()
    z = np.zeros_like(m)
    return jnp.asarray(np.block([[m, z], [z, m]]), dtype=BF16)


def _wkv_scan(r, w, k, a, b, c, v2, s0, C):
    G, T, npair, _ = r.shape
    rows = npair * A_HEAD_DIM
    assert C % 2 == 0
    my = _block_ones_pair()
    seq = pl.BlockSpec((1, C, npair, LANES), lambda g, c: (g, c, 0, 0))
    st = pl.BlockSpec((1, rows, LANES), lambda g, c: (g, 0, 0))
    return pl.pallas_call(
        _scan_kernel,
        grid=(G, T // C),
        in_specs=[seq] * 6 + [pl.BlockSpec((1, C, v2.shape[2]), lambda g, c: (g, c, 0)), st,
                              pl.BlockSpec(my.shape, lambda g, c: (0, 0))],
        out_specs=[pl.BlockSpec((1, 1, rows, LANES), lambda g, c: (g, c, 0, 0)), st],
        out_shape=[jax.ShapeDtypeStruct((G, T // C, rows, LANES), F32),
                   jax.ShapeDtypeStruct((G, rows, LANES), F32)],
        scratch_shapes=[pltpu.VMEM((rows, LANES), F32), pltpu.VMEM((rows, LANES), F32),
                        pltpu.VMEM((rows, 2 * LANES), BF16), pltpu.VMEM((rows, 2 * LANES), F32)],
        compiler_params=_cparams(("arbitrary", "arbitrary")),
        name="wkv_scan",
    )(r, w, k, a, b, c, v2, s0, my)


def _post_kernel(yt_ref, g_ref, bon_ref, lw_ref, lb_ref, m2_ref, o_ref):
    C = o_ref.shape[1]
    rows = yt_ref.shape[2]
    chunks = []
    for pp in range(rows // LANES):
        rt = yt_ref[0, 0, pp * LANES:(pp + 1) * LANES, :].T
        chunks.extend(o[:C] for o in _half_lane_unzip(rt))
    y = jnp.concatenate(chunks, axis=1)
    m2 = m2_ref[...]
    inv = 1.0 / A_HEAD_DIM
    mean = _seg_sum(y, m2) * inv
    yc = y - mean
    var = _seg_sum(yc * yc, m2) * inv
    yn = yc * lax.rsqrt(var + LNX_EPS) * lw_ref[...] + lb_ref[...]
    o_ref[0] = ((yn + bon_ref[0]) * g_ref[0]).astype(o_ref.dtype)


def _rwkv_post(yt, g, bon, lnx_w, lnx_b, m2):
    G, NC, rows, _ = yt.shape
    T, aw = g.shape[1], g.shape[2]
    C = T // NC
    big = pl.BlockSpec((1, C, aw), lambda a, c: (a, c, 0))
    vec = pl.BlockSpec((1, aw), lambda a, c: (0, 0))
    return pl.pallas_call(
        _post_kernel,
        grid=(G, NC),
        in_specs=[pl.BlockSpec((1, 1, rows, LANES), lambda a, c: (a, c, 0, 0)), big, big, vec, vec,
                  pl.BlockSpec(m2.shape, lambda a, c: (0, 0))],
        out_specs=big,
        out_shape=jax.ShapeDtypeStruct((G, T, aw), F32),
        compiler_params=_cparams(("arbitrary", "arbitrary")),
        name="rwkv_post",
    )(yt, g, bon, lnx_w, lnx_b, m2)


def _moba_prep_kernel(k_ref, v_ref, kb_ref, vt_ref, km_ref):
    k = k_ref[...]
    kb_ref[...] = k.astype(BF16)
    km_ref[0] = jnp.mean(k, axis=0, keepdims=True)
    vt_ref[...] = v_ref[...].T.astype(BF16)


def _moba_prep(k, v):
    T, W = k.shape
    nb = T // MOBA_BLOCK
    blk = pl.BlockSpec((MOBA_BLOCK, W), lambda n: (n, 0))
    return pl.pallas_call(
        _moba_prep_kernel,
        grid=(nb,),
        in_specs=[blk, blk],
        out_specs=[blk, pl.BlockSpec((W, MOBA_BLOCK), lambda n: (0, n)),
                   pl.BlockSpec((1, 1, W), lambda n: (n, 0, 0))],
        out_shape=[jax.ShapeDtypeStruct((T, W), BF16), jax.ShapeDtypeStruct((W, T), BF16),
                   jax.ShapeDtypeStruct((nb, 1, W), F32)],
        compiler_params=_cparams(("arbitrary",)),
        name="moba_prep",
    )(k, v)


def _select_topk(gate, valid, axis):
    idx = lax.broadcasted_iota(jnp.int32, gate.shape, axis)
    g = jnp.where(valid, gate, NEG)
    sel = jnp.zeros(gate.shape, jnp.bool_)
    for _ in range(MOBA_TOPK):
        m = jnp.max(g, axis=axis, keepdims=True)
        first = jnp.min(jnp.where(g == m, idx, jnp.int32(2 ** 30)), axis=axis, keepdims=True)
        pick = idx == first
        sel = jnp.logical_or(sel, pick)
        g = jnp.where(pick, -jnp.inf, g)
    return jnp.logical_and(sel, valid)


PROMPT_HEADS_PER_STEP = 2
PROMPT_GROUP = 8
LOG2E = 1.4426950408889634


def _moba_prompt_kernel(q_ref, k_ref, vt_ref, km_ref, o_ref, sel_ref, s_ref):
    i = pl.program_id(1)
    tq = q_ref.shape[0]
    nb = km_ref.shape[0]
    dh = B_HEAD_DIM
    nt = (((1,), (1,)), ((), ()))
    heads = range(PROMPT_HEADS_PER_STEP)
    blk = lax.broadcasted_iota(jnp.int32, (nb, tq), 0)
    key = lax.broadcasted_iota(jnp.int32, (MOBA_BLOCK, tq), 0)
    qi = lax.broadcasted_iota(jnp.int32, (MOBA_BLOCK, tq), 1)
    start = pl.multiple_of(i * MOBA_BLOCK, MOBA_BLOCK)

    qbs = []
    for u in heads:
        sl = slice(u * dh, (u + 1) * dh)
        q = q_ref[:, sl]
        qb = (q * (dh ** -0.5 * LOG2E)).astype(BF16)
        qbs.append(qb)
        gate_t = lax.dot_general(km_ref[:, sl], q, nt, precision=lax.Precision.HIGHEST,
                                 preferred_element_type=F32)
        sel_ref[u, 0:nb, :] = _select_topk(gate_t, blk < i, 0).astype(F32)
        sel_ref[u, nb:, :] = jnp.zeros((sel_ref.shape[1] - nb, tq), F32)

    def scores(n, u):
        st = pl.multiple_of(n * MOBA_BLOCK, MOBA_BLOCK)
        return lax.dot_general(k_ref[pl.ds(st, MOBA_BLOCK), u * dh:(u + 1) * dh], qbs[u], nt,
                               preferred_element_type=F32)

    def selected(n, u):
        return sel_ref[u, pl.ds(n, 1), :] > 0.5

    gk = PROMPT_GROUP * MOBA_BLOCK
    n_iter = (i + PROMPT_GROUP - 1) // PROMPT_GROUP

    def pass1(g, ms):
        st = pl.multiple_of(g * gk, gk)
        out = []
        for u in heads:
            m = ms[u]
            s_g = lax.dot_general(k_ref[pl.ds(st, gk), u * dh:(u + 1) * dh], qbs[u], nt,
                                  preferred_element_type=F32)
            s_ref[u, pl.ds(st, gk), :] = s_g
            for c in range(PROMPT_GROUP):
                mb = jnp.max(s_g[c * MOBA_BLOCK:(c + 1) * MOBA_BLOCK], axis=0, keepdims=True)
                m = jnp.maximum(m, jnp.where(selected(g * PROMPT_GROUP + c, u), mb, NEG))
            out.append(m)
        return tuple(out)

    ms = lax.fori_loop(0, n_iter, pass1, tuple(jnp.full((1, tq), NEG, F32) for _ in heads))

    init = []
    for u in heads:
        sl = slice(u * dh, (u + 1) * dh)
        s_t = jnp.where(key <= qi, scores(i, u), NEG)
        m = jnp.maximum(ms[u], jnp.max(s_t, axis=0, keepdims=True))
        p = jnp.exp2(s_t - m)
        l = jnp.sum(p, axis=0, keepdims=True)
        acc = jnp.dot(vt_ref[sl, pl.ds(start, MOBA_BLOCK)], p.astype(BF16), preferred_element_type=F32)
        init.append((m, l, acc))
    ms = tuple(m for m, _, _ in init)

    def pass2(g, carry):
        st = pl.multiple_of(g * gk, gk)
        out = []
        for u in heads:
            sl = slice(u * dh, (u + 1) * dh)
            l, acc = carry[u]
            ps = []
            for c in range(PROMPT_GROUP):
                n = g * PROMPT_GROUP + c
                s_t = s_ref[u, pl.ds(pl.multiple_of(n * MOBA_BLOCK, MOBA_BLOCK), MOBA_BLOCK), :]
                p = jnp.exp2(s_t - jnp.where(selected(n, u), ms[u], -NEG))
                l = l + jnp.sum(p, axis=0, keepdims=True)
                ps.append(p.astype(BF16))
            acc = acc + jnp.dot(vt_ref[sl, pl.ds(st, gk)], jnp.concatenate(ps, axis=0),
                                preferred_element_type=F32)
            out.append((l, acc))
        return tuple(out)

    fin = lax.fori_loop(0, n_iter, pass2, tuple((l, acc) for _, l, acc in init))
    for u in heads:
        l, acc = fin[u]
        o_ref[:, u * dh:(u + 1) * dh] = (acc / l).T


def _moba_prompt(q, kb, vt, kmean):
    T, W = q.shape
    hw = PROMPT_HEADS_PER_STEP * B_HEAD_DIM
    nb = T // MOBA_BLOCK
    tp = T if nb % PROMPT_GROUP == 0 else T + (PROMPT_GROUP - 1) * MOBA_BLOCK
    kb = jnp.pad(kb, ((0, tp - T), (0, 0)))
    vt = jnp.pad(vt, ((0, 0), (0, tp - T)))
    once = pl.Buffered(1)
    return pl.pallas_call(
        _moba_prompt_kernel,
        grid=(W // hw, nb),
        in_specs=[pl.BlockSpec((MOBA_BLOCK, hw), lambda h, i: (i, h)),
                  pl.BlockSpec((tp, hw), lambda h, i: (0, h), pipeline_mode=once),
                  pl.BlockSpec((hw, tp), lambda h, i: (h, 0), pipeline_mode=once),
                  pl.BlockSpec((nb, hw), lambda h, i: (0, h))],
        out_specs=pl.BlockSpec((MOBA_BLOCK, hw), lambda h, i: (i, h)),
        out_shape=jax.ShapeDtypeStruct((T, W), F32),
        scratch_shapes=[pltpu.VMEM((PROMPT_HEADS_PER_STEP, nb + SUBLANES, MOBA_BLOCK), F32),
                        pltpu.VMEM((PROMPT_HEADS_PER_STEP, tp, MOBA_BLOCK), F32)],
        compiler_params=pltpu.CompilerParams(dimension_semantics=("arbitrary", "arbitrary"),
                                             vmem_limit_bytes=VMEM_LIMIT_ATTN),
        name="moba_prompt",
    )(q, kb, vt, kmean)


SAMPLE_PAGES_PER_STEP = 16


def _moba_sample_kernel(pt_ref, q_ref, kn_ref, vn_ref, *rest):
    P = SAMPLE_PAGES_PER_STEP
    kp = rest[:P]
    vp = rest[P:2 * P]
    o_ref = rest[2 * P]
    km_ref, m_ref, l_ref, oacc_ref = rest[2 * P + 1:]
    s = pl.program_id(1)
    ns = pl.num_programs(1)
    tq, W = q_ref.shape[1], q_ref.shape[2]
    dh = B_HEAD_DIM
    nh = W // dh
    hq = nh * tq
    nbp = km_ref.shape[1]
    ppb = MOBA_BLOCK // PAGE_SIZE
    bps = P // ppb
    nt = (((1,), (1,)), ((), ()))

    def by_head(x):
        return jnp.concatenate([x[:, h * dh:(h + 1) * dh] for h in range(nh)], axis=0)

    qall = by_head(q_ref[0])
    qall_b = (qall * (dh ** -0.5)).astype(BF16)

    def partial(k_b, v_b, mask):
        sc = jnp.where(mask, lax.dot_general(qall_b, k_b, nt, preferred_element_type=F32), NEG)
        m = jnp.max(sc, axis=1, keepdims=True)
        p = jnp.exp(sc - m)
        l = jnp.sum(p, axis=1, keepdims=True)
        o = jnp.dot(p.astype(BF16), v_b, preferred_element_type=F32)
        return jnp.broadcast_to(m, (hq, dh)), jnp.broadcast_to(l, (hq, dh)), o

    nflat = MOBA_BLOCK * nh
    rowh = lax.broadcasted_iota(jnp.int32, (hq, nflat), 0) // tq
    colh = lax.broadcasted_iota(jnp.int32, (hq, nflat), 1) % nh
    same_head = rowh == colh
    for j in range(bps):
        kpages = [kp[j * ppb + u][0, 0] for u in range(ppb)]
        vpages = [vp[j * ppb + u][0, 0] for u in range(ppb)]
        n = s * bps + j
        kmean = sum(jnp.sum(pg, axis=0) for pg in kpages) * (1.0 / MOBA_BLOCK)
        for h in range(nh):
            km_ref[h, pl.ds(n, 1), :] = kmean[h:h + 1, :]
        kflat = jnp.concatenate([pg.reshape(PAGE_SIZE * nh, dh) for pg in kpages], axis=0).astype(BF16)
        vflat = jnp.concatenate([pg.reshape(PAGE_SIZE * nh, dh) for pg in vpages], axis=0).astype(BF16)
        m, l, o = partial(kflat, vflat, same_head)
        m_ref[n] = m
        l_ref[n] = l
        oacc_ref[n] = o

    @pl.when(s == ns - 1)
    def _():
        pad = jnp.zeros((LANES - hq, dh), F32)
        kn = jnp.concatenate([by_head(kn_ref[0]), pad], axis=0).astype(BF16)
        vn = jnp.concatenate([by_head(vn_ref[0]), pad], axis=0).astype(BF16)
        col = lax.broadcasted_iota(jnp.int32, (hq, LANES), 1)
        row = lax.broadcasted_iota(jnp.int32, (hq, LANES), 0)
        own_mask = jnp.logical_and(col // tq == row // tq, col % tq <= row % tq)
        m_own, l_own, o_own = partial(kn, vn, own_mask)

        gate = jnp.concatenate(
            [lax.dot_general(qall[h * tq:(h + 1) * tq], km_ref[h], nt, precision=lax.Precision.HIGHEST,
                             preferred_element_type=F32) for h in range(nh)], axis=0)
        sel = _select_topk(gate, jnp.ones(gate.shape, jnp.bool_), 1).astype(F32)
        mx = m_own
        for n in range(nbp):
            sn = sel[:, n:n + 1] > 0.5
            mx = jnp.where(sn, jnp.maximum(mx, m_ref[n]), mx)
        w_own = jnp.exp(m_own - mx)
        lsum = w_own * l_own
        osum = w_own * o_own
        for n in range(nbp):
            sn = sel[:, n:n + 1] > 0.5
            wn = jnp.where(sn, jnp.exp(m_ref[n] - mx), 0.0)
            lsum = lsum + wn * l_ref[n]
            osum = osum + wn * oacc_ref[n]
        o_ref[0] = osum / lsum


def _moba_sample(q, kn, vn, cache_k, cache_v, page_table):
    B, tq, W = q.shape
    nh = W // B_HEAD_DIM
    hq = nh * tq
    assert hq <= LANES
    n_pages = page_table.shape[1]
    P = SAMPLE_PAGES_PER_STEP
    nbp = n_pages * PAGE_SIZE // MOBA_BLOCK
    small = pl.BlockSpec((1, tq, W), lambda b, s, pt: (b, 0, 0))

    def page_spec(j):
        return pl.BlockSpec((1, 1, PAGE_SIZE, nh, B_HEAD_DIM), lambda b, s, pt: (0, pt[b, s * P + j], 0, 0, 0))

    grid_spec = pltpu.PrefetchScalarGridSpec(
        num_scalar_prefetch=1,
        grid=(B, n_pages // P),
        in_specs=[small, small, small] + [page_spec(j) for j in range(P)] * 2,
        out_specs=pl.BlockSpec((1, hq, B_HEAD_DIM), lambda b, s, pt: (b, 0, 0)),
        scratch_shapes=[pltpu.VMEM((nh, nbp, B_HEAD_DIM), F32),
                        pltpu.VMEM((nbp, hq, B_HEAD_DIM), F32), pltpu.VMEM((nbp, hq, B_HEAD_DIM), F32),
                        pltpu.VMEM((nbp, hq, B_HEAD_DIM), F32)],
    )
    return pl.pallas_call(
        _moba_sample_kernel,
        grid_spec=grid_spec,
        out_shape=jax.ShapeDtypeStruct((B, hq, B_HEAD_DIM), F32),
        compiler_params=_cparams(("arbitrary", "arbitrary")),
        name="moba_sample",
    )(page_table, q, kn, vn, *([cache_k] * P), *([cache_v] * P))


def _mixout_kernel(x_ref, oa_ref, ob_ref, pg_ref, gt_ref, gp_ref, wa_ref, wb_ref, o_ref):
    gb, tm, d = x_ref.shape
    aw = oa_ref.shape[2]
    pg = pg_ref[...]
    ma = (oa_ref[...] * jax.nn.sigmoid(pg[..., :aw])).reshape(gb * tm, aw).astype(BF16)
    mb = (ob_ref[...] * jax.nn.sigmoid(pg[..., aw:])).reshape(gb * tm, pg.shape[2] - aw).astype(BF16)
    f = jnp.dot(ma, wa_ref[...], preferred_element_type=F32) + jnp.dot(mb, wb_ref[...], preferred_element_type=F32)
    y = f * lax.rsqrt(jnp.mean(f * f, axis=-1, keepdims=True) + RMS_EPS)
    y = y.reshape(gb, tm, d) * gp_ref[...]
    o_ref[...] = x_ref[...] + gt_ref[...] * y


def _mixout(x, oa, ob, pg, gt, gpost, wa, wb, gb, tm):
    G, T, D = x.shape
    aw, bw = oa.shape[2], ob.shape[2]
    row = lambda n: pl.BlockSpec((gb, tm, n), lambda a, i: (a, i, 0))
    return pl.pallas_call(
        _mixout_kernel,
        grid=(G // gb, T // tm),
        in_specs=[row(D), row(aw), row(bw), row(aw + bw),
                  pl.BlockSpec((gb, 1, D), lambda a, i: (a, 0, 0)),
                  pl.BlockSpec((1, 1, D), lambda a, i: (0, 0, 0)),
                  pl.BlockSpec(wa.shape, lambda a, i: (0, 0)),
                  pl.BlockSpec(wb.shape, lambda a, i: (0, 0))],
        out_specs=row(D),
        out_shape=jax.ShapeDtypeStruct((G, T, D), F32),
        compiler_params=_cparams(("arbitrary", "arbitrary")),
        name="mix_out",
    )(x, oa, ob, pg, gt, gpost, wa, wb)


def _ffn_kernel(x_ref, gpre_ref, sc_ref, sh_ref, gt_ref, gpost_ref, wg_ref, wu_ref, wo_ref, o_ref, h_ref):
    gb, tm, d = x_ref.shape
    j = pl.program_id(2)

    @pl.when(j == 0)
    def _():
        h = _norm_mod(x_ref[...], gpre_ref[...], sc_ref[...], sh_ref[...])
        h_ref[...] = h.reshape(gb * tm, d).astype(BF16)
        o_ref[...] = jnp.zeros(o_ref.shape, F32)

    h = h_ref[...]
    g = jnp.dot(h, wg_ref[...], preferred_element_type=F32)
    u = jnp.dot(h, wu_ref[...], preferred_element_type=F32)
    act = (g * jax.nn.sigmoid(g) * u).astype(BF16)
    o_ref[...] += jnp.dot(act, wo_ref[...], preferred_element_type=F32).reshape(gb, tm, d)

    @pl.when(j == pl.num_programs(2) - 1)
    def _():
        f = o_ref[...]
        y = f * lax.rsqrt(jnp.mean(f * f, axis=-1, keepdims=True) + RMS_EPS)
        o_ref[...] = x_ref[...] + gt_ref[...] * (y * gpost_ref[...])


def _ffn(x, gpre, sc, sh, gt, gpost, w_in, w_out, gb, tm, tf):
    G, T, D = x.shape
    dff = w_out.shape[0]
    nj = dff // tf
    if gb == 1 and T % (2 * tm) == 0:
        tm = 2 * tm
    row = pl.BlockSpec((gb, tm, D), lambda a, i, j: (a, i, 0))
    row_once = pl.BlockSpec((gb, tm, D), lambda a, i, j: (a, i, 0), pipeline_mode=pl.Buffered(1))
    mod = pl.BlockSpec((gb, 1, D), lambda a, i, j: (a, 0, 0))
    gain = pl.BlockSpec((1, 1, D), lambda a, i, j: (0, 0, 0))
    return pl.pallas_call(
        _ffn_kernel,
        grid=(G // gb, T // tm, nj),
        in_specs=[row_once, gain, mod, mod, mod, gain,
                  pl.BlockSpec((D, tf), lambda a, i, j: (0, j)),
                  pl.BlockSpec((D, tf), lambda a, i, j: (0, j + nj)),
                  pl.BlockSpec((tf, D), lambda a, i, j: (j, 0))],
        out_specs=row,
        out_shape=jax.ShapeDtypeStruct((G, T, D), F32),
        scratch_shapes=[pltpu.VMEM((gb * tm, D), BF16)],
        compiler_params=_cparams(("arbitrary", "arbitrary", "arbitrary")),
        name="ffn",
    )(x, gpre, sc, sh, gt, gpost, w_in, w_in, w_out)


def _pair_state(s):
    B, H, n, _ = s.shape
    return s.reshape(B, H // 2, 2, n, n).transpose(0, 1, 3, 2, 4).reshape(B, (H // 2) * n, 2 * n)


def _unpair_state(s, H):
    B = s.shape[0]
    n = A_HEAD_DIM
    return s.reshape(B, H // 2, n, 2, n).transpose(0, 1, 3, 2, 4).reshape(B, H, n, n)


def _trunk(x, mod, shift0, wkv0, moba_fn, W, gb, tm):
    G, T, D = x.shape
    aw = W["w0"].shape[-1]
    bw = D - aw
    ac = W["mu"].shape[-1]
    m2 = W["m2"]
    md = [mod[:, c:c + 1, :] for c in range(6)]
    sh_m, sc_m, gt_m, sh_f, sc_f, gt_f = md

    h = _norm_mod_rows(x, W["g_pre_mix"], sc_m, sh_m, gb, tm)
    tmm = 2 * tm if T % (2 * tm) == 0 else tm
    proj = lambda w, tn: _matmul(h, w, gb, tmm, tn)
    pa = proj(W["w_in_a"], ac // 2)
    q = proj(W["w_in_q"], bw)
    k = proj(W["w_in_k"], bw)
    v = proj(W["w_in_v"], bw)
    pg = proj(W["w_in_g"], 1024)

    tp = min(T, 256)
    r_, w_, k_, v_, a_, b_, c_, g_, bon, last = _rwkv_prep(
        pa, shift0[:, None, :], W["mu"], W["w0"], W["a0"], W["k_k"], W["k_a"], W["r_k"], W["wwa"], W["g2"], m2, tp)
    C = min(T, A_HEAD_DIM)
    npair = aw // LANES
    rs = lambda z: z.reshape(G, T, npair, LANES)
    yt, s_fin = _wkv_scan(rs(r_), rs(w_), rs(k_), rs(a_), rs(b_), rs(c_), v_, _pair_state(wkv0), C)
    o_a = _rwkv_post(yt, g_, bon, W["lnx_w"], W["lnx_b"], m2)

    o_b = moba_fn(q, k, v)

    x1 = _mixout(x, o_a, o_b, pg, gt_m, W["g_post_mix"], W["w_out_a"], W["w_out_b"], gb, min(tm, 256))
    y = _ffn(x1, W["g_pre_ffn"], sc_f, sh_f, gt_f, W["g_post_ffn"], W["w_ffn_in"], W["w_ffn_out"], gb, tm, 512)
    return y, last[:, 0, :], _unpair_state(s_fin, aw // A_HEAD_DIM), k, v


def kernel(x_prompt, x_sample, cache_k, cache_v, page_table, state_wkv, state_shift, c_prompt, c_sample, w_ada, b_ada, g_pre_mix, g_post_mix, g_pre_ffn, g_post_ffn, w_in, mu_shift, w0, w2, a0, a2, g2, k_k, k_a, r_k, lnx_w, lnx_b, w_out, w_ffn_in, w_ffn_out):
    depth = w_in.shape[0]
    assert depth == 1
    l = 0
    Bp, Tp, D = x_prompt.shape
    Bs, Ts, _ = x_sample.shape
    aw = w0.shape[-1]
    bw = D - aw
    ac = mu_shift.shape[-1]
    nh_a = aw // A_HEAD_DIM
    nh_b = bw // B_HEAD_DIM
    assert Bp == 1

    nc = Bp + Bs
    ncp = -(-nc // SUBLANES) * SUBLANES
    c_all = jnp.concatenate([c_prompt, c_sample, jnp.zeros((ncp - nc, D), F32)], axis=0)
    mod = _ada(c_all, w_ada[l], b_ada[l]).reshape(ncp, 6, D)
    mod_p, mod_s = mod[:Bp], mod[Bp:nc]

    wi = w_in[l]
    zeros_wa = jnp.zeros((DECAY_LORA, aw), F32)
    wwa = jnp.concatenate([jnp.concatenate([w2[l], zeros_wa], axis=1),
                           jnp.concatenate([zeros_wa, a2[l]], axis=1)], axis=0).astype(BF16)
    W = dict(
        m2=_block_ones(),
        g_pre_mix=g_pre_mix[l].reshape(1, 1, D), g_post_mix=g_post_mix[l].reshape(1, 1, D),
        g_pre_ffn=g_pre_ffn[l].reshape(1, 1, D), g_post_ffn=g_post_ffn[l].reshape(1, 1, D),
        w_in_a=wi[:, :ac].astype(BF16),
        w_in_q=wi[:, ac:ac + bw].astype(BF16),
        w_in_k=wi[:, ac + bw:ac + 2 * bw].astype(BF16),
        w_in_v=wi[:, ac + 2 * bw:ac + 3 * bw].astype(BF16),
        w_in_g=wi[:, ac + 3 * bw:].astype(BF16),
        mu=mu_shift[l].reshape(1, ac), w0=w0[l].reshape(1, aw), a0=a0[l].reshape(1, aw),
        k_k=k_k[l].reshape(1, aw), k_a=k_a[l].reshape(1, aw), r_k=r_k[l].reshape(1, aw),
        wwa=wwa, g2=g2[l].astype(BF16),
        lnx_w=lnx_w[l].reshape(1, aw), lnx_b=lnx_b[l].reshape(1, aw),
        w_out_a=w_out[l][:aw].astype(BF16), w_out_b=w_out[l][aw:].astype(BF16),
        w_ffn_in=w_ffn_in[l].astype(BF16), w_ffn_out=w_ffn_out[l].astype(BF16),
    )

    def moba_prompt_fn(q, k, v):
        kb, vt, km = _moba_prep(k[0], v[0])
        return _moba_prompt(q[0], kb, vt, km[:, 0, :])[None]

    def moba_sample_fn(q, k, v):
        o = _moba_sample(q, k, v, cache_k, cache_v, page_table)
        return o.reshape(Bs, nh_b, Ts, B_HEAD_DIM).transpose(0, 2, 1, 3).reshape(Bs, Ts, bw)

    zero_shift = jnp.zeros((Bp, ac), F32)
    zero_wkv = jnp.zeros((Bp, nh_a, A_HEAD_DIM, A_HEAD_DIM), F32)
    yp, shp, wkvp, kp, vp = _trunk(x_prompt, mod_p, zero_shift, zero_wkv, moba_prompt_fn, W, 1, 512)
    ys, shs, wkvs, ksm, vsm = _trunk(x_sample, mod_s, state_shift[l], state_wkv[l], moba_sample_fn, W, Bs, Ts)

    hd = lambda z, B, T: z.reshape(1, B, T, nh_b, B_HEAD_DIM)
    return (yp, ys, hd(kp, Bp, Tp), hd(vp, Bp, Tp), wkvp[None], shp[None],
            hd(ksm, Bs, Ts), hd(vsm, Bs, Ts), wkvs[None], shs[None])
```
